```python
import jax, jax.numpy as jnp
from jax import lax
import numpy as np

D_MODEL = 1024
BATCH = 8
SEQ = 2048
DEPTH = 2

N_A = DEPTH // 2
N_B = DEPTH - N_A
CHUNK = 128
A_GROUPS = 8
A_GROUP_DIM = D_MODEL // A_GROUPS
N_HEADS = 16
HEAD_DIM = D_MODEL // N_HEADS
Q_BLOCK = 128
D_FF = 4 * D_MODEL
PLE_DIM = 256
EPS = 1e-6

kernel_name = "yoco_gmlp_stickbreaking_hybrid"


def rms_norm(x, g):
    xf = x.astype(jnp.float32)
    y = xf * lax.rsqrt(jnp.mean(xf * xf, axis=-1, keepdims=True) + EPS)
    return (y * g.astype(jnp.float32)).astype(x.dtype)


def sgu_mixer(h, w_in, g_v, w_s, b_s, w_out):
    bsz, seq, _ = h.shape
    z = jax.nn.gelu(h @ w_in)
    u, v = jnp.split(z, 2, axis=-1)
    v = rms_norm(v, g_v)
    v = v.reshape(bsz, seq // CHUNK, CHUNK, A_GROUPS, A_GROUP_DIM)
    causal = jnp.tril(jnp.ones((CHUNK, CHUNK), dtype=w_s.dtype))
    w = w_s * causal[None]
    mix = jnp.einsum('gts,bcsgd->bctgd', w, v) + jnp.transpose(b_s)[None, None, :, :, None]
    y = u * mix.reshape(bsz, seq, D_MODEL)
    return y @ w_out


def sqrelu_mlp(h, w_up, w_down):
    a = jax.nn.relu(h @ w_up)
    return (a * a) @ w_down


def shared_kv(x, ln_kv, w_kv, g_k):
    bsz, seq, _ = x.shape
    h = rms_norm(x, ln_kv)
    k, v = jnp.split(h @ w_kv, 2, axis=-1)
    k = rms_norm(k.reshape(bsz, seq, N_HEADS, HEAD_DIM), g_k)
    v = v.reshape(bsz, seq, N_HEADS, HEAD_DIM)
    return jnp.transpose(k, (0, 2, 1, 3)), jnp.transpose(v, (0, 2, 1, 3))


def stick_breaking(q, k, v):
    seq = q.shape[2]
    scale = HEAD_DIM ** -0.5
    outs = []
    for blk in range(seq // Q_BLOCK):
        t0 = blk * Q_BLOCK
        t1 = t0 + Q_BLOCK
        qb = q[:, :, t0:t1].astype(jnp.float32)
        kb = k[:, :, :t1].astype(jnp.float32)
        vb = v[:, :, :t1].astype(jnp.float32)
        z = jnp.einsum('bhqd,bhkd->bhqk', qb, kb) * scale
        q_idx = t0 + jnp.arange(Q_BLOCK)[:, None]
        k_idx = jnp.arange(t1)[None, :]
        causal = k_idx < q_idx
        log_1m_beta = jnp.where(causal, jax.nn.log_sigmoid(-z), 0.0)
        between = lax.cumsum(log_1m_beta, axis=3, reverse=True) - log_1m_beta
        a = jnp.where(causal, jnp.exp(jax.nn.log_sigmoid(z) + between), 0.0)
        o = jnp.einsum('bhqk,bhkd->bhqd', a, vb)
        outs.append(o.astype(v.dtype))
    return jnp.concatenate(outs, axis=2)


def stick_breaking_mixer(h, w_q, g_q, k, v, w_out):
    bsz, seq, _ = h.shape
    q = rms_norm((h @ w_q).reshape(bsz, seq, N_HEADS, HEAD_DIM), g_q)
    q = jnp.transpose(q, (0, 2, 1, 3))
    o = stick_breaking(q, k, v)
    o = jnp.transpose(o, (0, 2, 1, 3)).reshape(bsz, seq, D_MODEL)
    return o @ w_out


def setup_inputs(seed: int = 0) -> dict:
    key = jax.random.key(seed)
    ks = jax.random.split(key, 32)

    def nrm(k, shape, scale):
        return jax.random.normal(k, shape, dtype=jnp.float32) * scale

    def gain(k, shape):
        return 1.0 + nrm(k, shape, 0.02)

    return {
        "x": nrm(ks[0], (BATCH, SEQ, D_MODEL), 1.0),
        "p": nrm(ks[1], (DEPTH, BATCH, SEQ, PLE_DIM), 1.0),
        "ln_mix_a": gain(ks[2], (N_A, D_MODEL)),
        "w_in_a": nrm(ks[3], (N_A, D_MODEL, 2 * D_MODEL), D_MODEL ** -0.5),
        "g_v_a": gain(ks[4], (N_A, D_MODEL)),
        "w_spatial": nrm(ks[5], (N_A, A_GROUPS, CHUNK, CHUNK), CHUNK ** -0.5),
        "b_spatial": 1.0 + nrm(ks[6], (N_A, A_GROUPS, CHUNK), 0.02),
        "w_out_a": nrm(ks[7], (N_A, D_MODEL, D_MODEL), D_MODEL ** -0.5),
        "ln_kv": gain(ks[8], (D_MODEL,)),
        "w_kv": nrm(ks[9], (D_MODEL, 2 * D_MODEL), D_MODEL ** -0.5),
        "g_k": gain(ks[10], (HEAD_DIM,)),
        "ln_mix_b": gain(ks[11], (N_B, D_MODEL)),
        "w_q": nrm(ks[12], (N_B, D_MODEL, D_MODEL), D_MODEL ** -0.5),
        "g_q": gain(ks[13], (N_B, HEAD_DIM)),
        "w_out_b": nrm(ks[14], (N_B, D_MODEL, D_MODEL), D_MODEL ** -0.5),
        "ln_mlp": gain(ks[15], (DEPTH, D_MODEL)),
        "w_up": nrm(ks[16], (DEPTH, D_MODEL, D_FF), D_MODEL ** -0.5),
        "w_down": nrm(ks[17], (DEPTH, D_FF, D_MODEL), D_FF ** -0.5),
        "ln_ple": gain(ks[18], (DEPTH, D_MODEL)),
        "w_ple_gate": nrm(ks[19], (DEPTH, D_MODEL, D_MODEL), D_MODEL ** -0.5),
        "w_ple_proj": nrm(ks[20], (DEPTH, PLE_DIM, D_MODEL), PLE_DIM ** -0.5),
    }


def reference(x, p, ln_mix_a, w_in_a, g_v_a, w_spatial, b_spatial, w_out_a,
              ln_kv, w_kv, g_k, ln_mix_b, w_q, g_q, w_out_b,
              ln_mlp, w_up, w_down, ln_ple, w_ple_gate, w_ple_proj):
    k_shared = None
    v_shared = None
    for i in range(DEPTH):
        if i < N_A:
            h = rms_norm(x, ln_mix_a[i])
            x = x + sgu_mixer(h, w_in_a[i], g_v_a[i], w_spatial[i], b_spatial[i], w_out_a[i])
        else:
            j = i - N_A
            h = rms_norm(x, ln_mix_b[j])
            x = x + stick_breaking_mixer(h, w_q[j], g_q[j], k_shared, v_shared, w_out_b[j])
        x = x + sqrelu_mlp(rms_norm(x, ln_mlp[i]), w_up[i], w_down[i])
        gate = jax.nn.sigmoid(rms_norm(x, ln_ple[i]) @ w_ple_gate[i])
        x = x + (p[i] @ w_ple_proj[i]) * gate
        if i == N_A - 1:
            k_shared, v_shared = shared_kv(x, ln_kv, w_kv, g_k)
    return x
```

```python
import functools

import jax
import jax.numpy as jnp
from jax import lax
from jax.experimental import pallas as pl
from jax.experimental.pallas import tpu as pltpu

EPS = 1e-6
CHUNK = 128
HEAD_DIM = 64
LANES = 128
TOKEN_TILE = 256
ATTN_BLOCK = 128
FF_CHUNK = 1024
VMEM_LIMIT = 48 * 1024 * 1024

_BF16 = jnp.bfloat16
_F32 = jnp.float32


def _rms(x, g):
    return x * lax.rsqrt(jnp.mean(x * x, axis=-1, keepdims=True) + EPS) * g


def _const_spec(shape):
    return pl.BlockSpec(shape, lambda *_: (0,) * len(shape), pipeline_mode=pl.Buffered(1))


def _params(n_axes):
    return pltpu.CompilerParams(
        dimension_semantics=("arbitrary",) * n_axes, vmem_limit_bytes=VMEM_LIMIT)


def _sgu_kernel(x_ref, ln_ref, win_ref, gv_ref, ws_ref, bias_ref, y_ref):
    tm, d = x_ref.shape
    n_groups = ws_ref.shape[0]
    n_chunks = tm // CHUNK
    h = _rms(x_ref[...], ln_ref[...]).astype(_BF16)
    z = jax.nn.gelu(jnp.dot(h, win_ref[...], preferred_element_type=_F32))
    u = z[:, :d]
    v = _rms(z[:, d:], gv_ref[...]).astype(_BF16)
    row = lax.broadcasted_iota(jnp.int32, (CHUNK, CHUNK), 0)
    col = lax.broadcasted_iota(jnp.int32, (CHUNK, CHUNK), 1)
    causal = col <= row
    mix_cols = []
    for g in range(n_groups):
        w = jnp.where(causal, ws_ref[g], 0.0).astype(_BF16)
        vg = jnp.concatenate(
            [v[c * CHUNK:(c + 1) * CHUNK, g * LANES:(g + 1) * LANES] for c in range(n_chunks)],
            axis=1)
        mg = jnp.dot(w, vg, preferred_element_type=_F32)
        mix_cols.append(jnp.concatenate(
            [mg[:, c * LANES:(c + 1) * LANES] for c in range(n_chunks)], axis=0))
    mix = jnp.concatenate(mix_cols, axis=1)
    bias = jnp.concatenate([bias_ref[...]] * n_chunks, axis=0)
    y_ref[...] = (u * (mix + bias)).astype(y_ref.dtype)


def _sgu_call(x, ln, w_in, g_v, w_s, bias):
    m, d = x.shape
    tm = TOKEN_TILE
    return pl.pallas_call(
        _sgu_kernel,
        grid=(m // tm,),
        in_specs=[
            pl.BlockSpec((tm, d), lambda i: (i, 0)),
            _const_spec((1, d)),
            _const_spec(w_in.shape),
            _const_spec((1, d)),
            _const_spec(w_s.shape),
            _const_spec(bias.shape),
        ],
        out_specs=pl.BlockSpec((tm, d), lambda i: (i, 0)),
        out_shape=jax.ShapeDtypeStruct((m, d), _BF16),
        compiler_params=_params(1),
        name="sgu_mixer",
    )(x, ln, w_in, g_v, w_s, bias)


def _post_kernel(x_ref, y_ref, p_ref, wo_ref, lnm_ref, wup_ref, wdn_ref,
                 lnp_ref, wg_ref, wp_ref, o_ref):
    d_ff = wup_ref.shape[1]
    x = x_ref[...] + jnp.dot(y_ref[...], wo_ref[...], preferred_element_type=_F32)
    h = _rms(x, lnm_ref[...]).astype(_BF16)
    acc = x
    for c in range(d_ff // FF_CHUNK):
        a = jnp.dot(h, wup_ref[:, c * FF_CHUNK:(c + 1) * FF_CHUNK], preferred_element_type=_F32)
        a = jnp.maximum(a, 0.0)
        a = (a * a).astype(_BF16)
        acc = acc + jnp.dot(a, wdn_ref[c * FF_CHUNK:(c + 1) * FF_CHUNK, :],
                            preferred_element_type=_F32)
    x = acc
    hg = _rms(x, lnp_ref[...]).astype(_BF16)
    gate = 1.0 / (1.0 + jnp.exp(-jnp.dot(hg, wg_ref[...], preferred_element_type=_F32)))
    pp = jnp.dot(p_ref[...].astype(_BF16), wp_ref[...], preferred_element_type=_F32)
    o_ref[...] = x + pp * gate


def _post_call(x, y, p, w_out, ln_mlp, w_up, w_down, ln_ple, w_gate, w_proj):
    m, d = x.shape
    tm = TOKEN_TILE
    return pl.pallas_call(
        _post_kernel,
        grid=(m // tm,),
        in_specs=[
            pl.BlockSpec((tm, d), lambda i: (i, 0)),
            pl.BlockSpec((tm, d), lambda i: (i, 0)),
            pl.BlockSpec((tm, p.shape[1]), lambda i: (i, 0)),
            _const_spec(w_out.shape),
            _const_spec((1, d)),
            _const_spec(w_up.shape),
            _const_spec(w_down.shape),
            _const_spec((1, d)),
            _const_spec(w_gate.shape),
            _const_spec(w_proj.shape),
        ],
        out_specs=pl.BlockSpec((tm, d), lambda i: (i, 0)),
        out_shape=jax.ShapeDtypeStruct((m, d), _F32),
        compiler_params=_params(1),
        name="post_mlp_ple",
    )(x, y, p, w_out, ln_mlp, w_up, w_down, ln_ple, w_gate, w_proj)


def _proj_kernel(x_ref, ln_ref, w_ref, g_ref, o_ref, *, n_norm, out_scale):
    tm = x_ref.shape[0]
    h = _rms(x_ref[...], ln_ref[...]).astype(_BF16)
    y = jnp.dot(h, w_ref[...], preferred_element_type=_F32)
    first_head = lax.broadcasted_iota(jnp.int32, (tm, LANES), 1) < HEAD_DIM
    for t in range(y.shape[1] // LANES):
        blk = y[:, t * LANES:(t + 1) * LANES]
        if t < n_norm:
            sq = blk * blk
            s_a = jnp.sum(jnp.where(first_head, sq, 0.0), axis=-1, keepdims=True)
            s_b = jnp.sum(jnp.where(first_head, 0.0, sq), axis=-1, keepdims=True)
            ms = jnp.where(first_head, s_a, s_b) * (1.0 / HEAD_DIM)
            blk = blk * lax.rsqrt(ms + EPS) * g_ref[...] * out_scale
        o_ref[:, t * LANES:(t + 1) * LANES] = blk.astype(o_ref.dtype)


def _proj_call(x, ln, w, g_pair, n_norm, out_scale):
    m, d = x.shape
    n = w.shape[1]
    tm = TOKEN_TILE
    return pl.pallas_call(
        functools.partial(_proj_kernel, n_norm=n_norm, out_scale=out_scale),
        grid=(m // tm,),
        in_specs=[
            pl.BlockSpec((tm, d), lambda i: (i, 0)),
            _const_spec((1, d)),
            _const_spec(w.shape),
            _const_spec((1, LANES)),
        ],
        out_specs=pl.BlockSpec((tm, n), lambda i: (i, 0)),
        out_shape=jax.ShapeDtypeStruct((m, n), _BF16),
        compiler_params=_params(1),
        name="head_proj",
    )(x, ln, w, g_pair)


def _attn_kernel(q_ref, k_ref, v_ref, o_ref):
    blk = ATTN_BLOCK
    i = pl.program_id(2)
    q = q_ref[...]
    lane = lax.broadcasted_iota(jnp.int32, (blk, LANES), 1)
    first_head = lane < HEAD_DIM
    q_heads = (jnp.where(first_head, q, jnp.zeros_like(q)),
               jnp.where(first_head, jnp.zeros_like(q), q))
    row = lax.broadcasted_iota(jnp.int32, (blk, blk), 0)
    col = lax.broadcasted_iota(jnp.int32, (blk, blk), 1)
    col_minus_row = col - row
    kk = lax.broadcasted_iota(jnp.int32, (blk, 2 * blk), 0)
    ss = lax.broadcasted_iota(jnp.int32, (blk, 2 * blk), 1)
    cum = jnp.where((kk > ss) | (ss >= blk), 1.0, 0.0).astype(_BF16)

    def body(t, carry):
        j = i - t
        start = pl.multiple_of(j * blk, blk)
        kb = k_ref[pl.ds(start, blk), :]
        vb = v_ref[pl.ds(start, blk), :]
        valid = col_minus_row < t * blk
        new = []
        for hd in range(2):
            later, acc = carry[2 * hd], carry[2 * hd + 1]
            z = lax.dot_general(q_heads[hd], kb, (((1,), (1,)), ((), ())),
                                preferred_element_type=_F32)
            l1m = jnp.where(valid, -(jnp.maximum(z, 0.0) + jnp.log1p(jnp.exp(-jnp.abs(z)))), 0.0)
            hi = l1m.astype(_BF16)
            lo = (l1m - hi.astype(_F32)).astype(_BF16)
            r = (jnp.dot(hi, cum, preferred_element_type=_F32)
                 + jnp.dot(lo, cum, preferred_element_type=_F32))
            between = r[:, :blk] + later
            a = jnp.exp(jnp.where(valid, z + l1m + between, -jnp.inf))
            acc = acc + jnp.dot(a.astype(_BF16), vb, preferred_element_type=_F32)
            new += [later + r[:, blk:], acc]
        return tuple(new)

    zero = jnp.zeros((blk, LANES), _F32)
    out = lax.fori_loop(0, i + 1, body, (zero, zero, zero, zero))
    o_ref[...] = jnp.where(first_head, out[1], out[3]).astype(o_ref.dtype)


def _attn_call(q, kv, batch, seq):
    m, d = q.shape
    blk = ATTN_BLOCK
    n_q = seq // blk
    n_pairs = d // LANES
    return pl.pallas_call(
        _attn_kernel,
        grid=(batch, n_pairs, n_q),
        in_specs=[
            pl.BlockSpec((blk, LANES), lambda b, h, i: (b * n_q + i, h)),
            pl.BlockSpec((seq, LANES), lambda b, h, i: (b, h)),
            pl.BlockSpec((seq, LANES), lambda b, h, i: (b, n_pairs + h)),
        ],
        out_specs=pl.BlockSpec((blk, LANES), lambda b, h, i: (b * n_q + i, h)),
        out_shape=jax.ShapeDtypeStruct((m, d), _BF16),
        compiler_params=_params(3),
        name="stick_breaking",
    )(q, kv, kv)


def kernel(x, p, ln_mix_a, w_in_a, g_v_a, w_spatial, b_spatial, w_out_a, ln_kv, w_kv, g_k,
           ln_mix_b, w_q, g_q, w_out_b, ln_mlp, w_up, w_down, ln_ple, w_ple_gate, w_ple_proj):
    batch, seq, d = x.shape
    depth = p.shape[0]
    n_a = ln_mix_a.shape[0]
    m = batch * seq
    xf = x.reshape(m, d)
    pf = p.reshape(depth, m, p.shape[-1])
    bf = lambda w: w.astype(_BF16)
    row = lambda g: g.reshape(1, -1)
    pair = lambda g: jnp.concatenate([g, g]).reshape(1, LANES)
    kv_shared = None
    for i in range(depth):
        if i < n_a:
            bias = jnp.repeat(b_spatial[i].T, d // b_spatial.shape[1], axis=1)
            y = _sgu_call(xf, row(ln_mix_a[i]), bf(w_in_a[i]), row(g_v_a[i]), w_spatial[i], bias)
            w_o = w_out_a[i]
        else:
            j = i - n_a
            q = _proj_call(xf, row(ln_mix_b[j]), bf(w_q[j]), pair(g_q[j]),
                           d // LANES, HEAD_DIM ** -0.5)
            y = _attn_call(q, kv_shared, batch, seq)
            w_o = w_out_b[j]
        xf = _post_call(xf, y, pf[i], bf(w_o), row(ln_mlp[i]), bf(w_up[i]), bf(w_down[i]),
                        row(ln_ple[i]), bf(w_ple_gate[i]), bf(w_ple_proj[i]))
        if i == n_a - 1:
            kv_shared = _proj_call(xf, row(ln_kv), bf(w_kv), pair(g_k), d // LANES, 1.0)
    return xf.reshape(batch, seq, d)
```

```python
import functools

import jax
import jax.numpy as jnp
from jax import lax
from jax.experimental import pallas as pl
from jax.experimental.pallas import tpu as pltpu

EPS = 1e-6
CHUNK = 128
HEAD_DIM = 64
LANES = 128
TOKEN_TILE = 256
ATTN_BLOCK = 128
FF_CHUNK = 1024
VMEM_LIMIT = 48 * 1024 * 1024
LOG_WEIGHT_FLOOR = -88.0

_BF16 = jnp.bfloat16
_F32 = jnp.float32


def _rms(x, g):
    return x * lax.rsqrt(jnp.mean(x * x, axis=-1, keepdims=True) + EPS) * g


def _const_spec(shape):
    return pl.BlockSpec(shape, lambda *_: (0,) * len(shape), pipeline_mode=pl.Buffered(1))


def _params(n_axes):
    return pltpu.CompilerParams(
        dimension_semantics=("arbitrary",) * n_axes, vmem_limit_bytes=VMEM_LIMIT)


def _sgu_kernel(x_ref, ln_ref, win_ref, gv_ref, ws_ref, bias_ref, y_ref):
    tm, d = x_ref.shape
    n_groups = ws_ref.shape[0]
    n_chunks = tm // CHUNK
    h = _rms(x_ref[...], ln_ref[...]).astype(_BF16)
    z = jax.nn.gelu(jnp.dot(h, win_ref[...], preferred_element_type=_F32))
    u = z[:, :d]
    v = _rms(z[:, d:], gv_ref[...]).astype(_BF16)
    row = lax.broadcasted_iota(jnp.int32, (CHUNK, CHUNK), 0)
    col = lax.broadcasted_iota(jnp.int32, (CHUNK, CHUNK), 1)
    causal = col <= row
    mix_cols = []
    for g in range(n_groups):
        w = jnp.where(causal, ws_ref[g], 0.0).astype(_BF16)
        vg = jnp.concatenate(
            [v[c * CHUNK:(c + 1) * CHUNK, g * LANES:(g + 1) * LANES] for c in range(n_chunks)],
            axis=1)
        mg = jnp.dot(w, vg, preferred_element_type=_F32)
        mix_cols.append(jnp.concatenate(
            [mg[:, c * LANES:(c + 1) * LANES] for c in range(n_chunks)], axis=0))
    mix = jnp.concatenate(mix_cols, axis=1)
    bias = jnp.concatenate([bias_ref[...]] * n_chunks, axis=0)
    y_ref[...] = (u * (mix + bias)).astype(y_ref.dtype)


def _sgu_call(x, ln, w_in, g_v, w_s, bias):
    m, d = x.shape
    tm = TOKEN_TILE
    return pl.pallas_call(
        _sgu_kernel,
        grid=(m // tm,),
        in_specs=[
            pl.BlockSpec((tm, d), lambda i: (i, 0)),
            _const_spec((1, d)),
            _const_spec(w_in.shape),
            _const_spec((1, d)),
            _const_spec(w_s.shape),
            _const_spec(bias.shape),
        ],
        out_specs=pl.BlockSpec((tm, d), lambda i: (i, 0)),
        out_shape=jax.ShapeDtypeStruct((m, d), _BF16),
        compiler_params=_params(1),
        name="sgu_mixer",
    )(x, ln, w_in, g_v, w_s, bias)


def _post_kernel(x_ref, y_ref, p_ref, wo_ref, lnm_ref, wup_ref, wdn_ref,
                 lnp_ref, wg_ref, wp_ref, o_ref):
    d_ff = wup_ref.shape[1]
    x = x_ref[...] + jnp.dot(y_ref[...], wo_ref[...], preferred_element_type=_F32)
    h = _rms(x, lnm_ref[...]).astype(_BF16)
    acc = x
    for c in range(d_ff // FF_CHUNK):
        a = jnp.dot(h, wup_ref[:, c * FF_CHUNK:(c + 1) * FF_CHUNK], preferred_element_type=_F32)
        a = jnp.maximum(a, 0.0)
        a = (a * a).astype(_BF16)
        acc = acc + jnp.dot(a, wdn_ref[c * FF_CHUNK:(c + 1) * FF_CHUNK, :],
                            preferred_element_type=_F32)
    x = acc
    hg = _rms(x, lnp_ref[...]).astype(_BF16)
    gate = 1.0 / (1.0 + jnp.exp(-jnp.dot(hg, wg_ref[...], preferred_element_type=_F32)))
    pp = jnp.dot(p_ref[...].astype(_BF16), wp_ref[...], preferred_element_type=_F32)
    o_ref[...] = x + pp * gate


def _post_call(x, y, p, w_out, ln_mlp, w_up, w_down, ln_ple, w_gate, w_proj):
    m, d = x.shape
    tm = TOKEN_TILE
    return pl.pallas_call(
        _post_kernel,
        grid=(m // tm,),
        in_specs=[
            pl.BlockSpec((tm, d), lambda i: (i, 0)),
            pl.BlockSpec((tm, d), lambda i: (i, 0)),
            pl.BlockSpec((tm, p.shape[1]), lambda i: (i, 0)),
            _const_spec(w_out.shape),
            _const_spec((1, d)),
            _const_spec(w_up.shape),
            _const_spec(w_down.shape),
            _const_spec((1, d)),
            _const_spec(w_gate.shape),
            _const_spec(w_proj.shape),
        ],
        out_specs=pl.BlockSpec((tm, d), lambda i: (i, 0)),
        out_shape=jax.ShapeDtypeStruct((m, d), _F32),
        compiler_params=_params(1),
        name="post_mlp_ple",
    )(x, y, p, w_out, ln_mlp, w_up, w_down, ln_ple, w_gate, w_proj)


def _proj_kernel(x_ref, ln_ref, w_ref, g_ref, o_ref, *, n_norm, out_scale):
    tm = x_ref.shape[0]
    h = _rms(x_ref[...], ln_ref[...]).astype(_BF16)
    y = jnp.dot(h, w_ref[...], preferred_element_type=_F32)
    first_head = lax.broadcasted_iota(jnp.int32, (tm, LANES), 1) < HEAD_DIM
    for t in range(y.shape[1] // LANES):
        blk = y[:, t * LANES:(t + 1) * LANES]
        if t < n_norm:
            sq = blk * blk
            s_a = jnp.sum(jnp.where(first_head, sq, 0.0), axis=-1, keepdims=True)
            s_b = jnp.sum(jnp.where(first_head, 0.0, sq), axis=-1, keepdims=True)
            ms = jnp.where(first_head, s_a, s_b) * (1.0 / HEAD_DIM)
            blk = blk * lax.rsqrt(ms + EPS) * g_ref[...] * out_scale
        o_ref[:, t * LANES:(t + 1) * LANES] = blk.astype(o_ref.dtype)


def _proj_call(x, ln, w, g_pair, n_norm, out_scale):
    m, d = x.shape
    n = w.shape[1]
    tm = TOKEN_TILE
    return pl.pallas_call(
        functools.partial(_proj_kernel, n_norm=n_norm, out_scale=out_scale),
        grid=(m // tm,),
        in_specs=[
            pl.BlockSpec((tm, d), lambda i: (i, 0)),
            _const_spec((1, d)),
            _const_spec(w.shape),
            _const_spec((1, LANES)),
        ],
        out_specs=pl.BlockSpec((tm, n), lambda i: (i, 0)),
        out_shape=jax.ShapeDtypeStruct((m, n), _BF16),
        compiler_params=_params(1),
        name="head_proj",
    )(x, ln, w, g_pair)


def _attn_kernel(q_ref, k_ref, v_ref, o_ref, qs_ref, later_ref, acc_ref):
    blk = ATTN_BLOCK
    n_pairs = qs_ref.shape[0]
    i = pl.program_id(1)
    lane = lax.broadcasted_iota(jnp.int32, (blk, LANES), 1)
    first_head = lane < HEAD_DIM
    for p in range(n_pairs):
        q2 = q_ref[:, p * LANES:(p + 1) * LANES]
        zeros = jnp.zeros_like(q2)
        qs_ref[p] = jnp.concatenate(
            [jnp.where(first_head, q2, zeros), jnp.where(first_head, zeros, q2)], axis=0)
    later_ref[...] = jnp.zeros_like(later_ref)
    acc_ref[...] = jnp.zeros_like(acc_ref)

    kk = lax.broadcasted_iota(jnp.int32, (blk, 2 * blk), 0)
    ss = lax.broadcasted_iota(jnp.int32, (blk, 2 * blk), 1)
    cum = jnp.where((kk > ss) | (ss >= blk), 1.0, 0.0).astype(_BF16)

    def sweep_block(j, diagonal):
        start = pl.multiple_of(j * blk, blk)
        if diagonal:
            row = lax.broadcasted_iota(jnp.int32, (2 * blk, blk), 0) & (blk - 1)
            col = lax.broadcasted_iota(jnp.int32, (2 * blk, blk), 1)
            valid = col < row
        pairs = range(n_pairs)
        z = [lax.dot_general(qs_ref[p], k_ref[pl.ds(start, blk), p * LANES:(p + 1) * LANES],
                             (((1,), (1,)), ((), ())), preferred_element_type=_F32)
             for p in pairs]
        l1m = [-(jnp.maximum(zp, 0.0) + jnp.log1p(jnp.exp(-jnp.abs(zp)))) for zp in z]
        if diagonal:
            l1m = [jnp.where(valid, lp, 0.0) for lp in l1m]
        hi = [lp.astype(_BF16) for lp in l1m]
        lo = [(lp - hp.astype(_F32)).astype(_BF16) for lp, hp in zip(l1m, hi)]
        r = [jnp.dot(hp, cum, preferred_element_type=_F32)
             + jnp.dot(lp, cum, preferred_element_type=_F32) for hp, lp in zip(hi, lo)]
        log_a = [z[p] + l1m[p] + r[p][:, :blk] + later_ref[p] for p in pairs]
        if diagonal:
            log_a = [jnp.where(valid, x, -jnp.inf) for x in log_a]
        a = [jnp.exp(x).astype(_BF16) for x in log_a]
        bound = None
        for p in pairs:
            acc_ref[p] += jnp.dot(a[p], v_ref[pl.ds(start, blk), p * LANES:(p + 1) * LANES],
                                  preferred_element_type=_F32)
            later = later_ref[p] + r[p][:, blk:]
            later_ref[p] = later
            bound = later if bound is None else jnp.maximum(bound, later)
        return jnp.max(bound)

    bound0 = sweep_block(i, diagonal=True)

    def cond(carry):
        j, bound = carry
        return jnp.logical_and(j >= 0, bound > LOG_WEIGHT_FLOOR)

    def body(carry):
        j, _ = carry
        return j - 1, sweep_block(j, diagonal=False)

    lax.while_loop(cond, body, (i - 1, bound0))
    for p in range(n_pairs):
        acc = acc_ref[p]
        o_ref[:, p * LANES:(p + 1) * LANES] = jnp.where(
            first_head, acc[:blk], acc[blk:]).astype(o_ref.dtype)


def _attn_call(q, kv, batch, seq):
    m, d = q.shape
    blk = ATTN_BLOCK
    n_q = seq // blk
    n_pairs = d // LANES
    return pl.pallas_call(
        _attn_kernel,
        grid=(batch, n_q),
        in_specs=[
            pl.BlockSpec((blk, d), lambda b, i: (b * n_q + i, 0)),
            pl.BlockSpec((seq, d), lambda b, i: (b, 0)),
            pl.BlockSpec((seq, d), lambda b, i: (b, 1)),
        ],
        out_specs=pl.BlockSpec((blk, d), lambda b, i: (b * n_q + i, 0)),
        out_shape=jax.ShapeDtypeStruct((m, d), _BF16),
        scratch_shapes=[
            pltpu.VMEM((n_pairs, 2 * blk, LANES), _BF16),
            pltpu.VMEM((n_pairs, 2 * blk, LANES), _F32),
            pltpu.VMEM((n_pairs, 2 * blk, LANES), _F32),
        ],
        compiler_params=_params(2),
        name="stick_breaking",
    )(q, kv, kv)


def kernel(x, p, ln_mix_a, w_in_a, g_v_a, w_spatial, b_spatial, w_out_a, ln_kv, w_kv, g_k,
           ln_mix_b, w_q, g_q, w_out_b, ln_mlp, w_up, w_down, ln_ple, w_ple_gate, w_ple_proj):
    batch, seq, d = x.shape
    depth = p.shape[0]
    n_a = ln_mix_a.shape[0]
    m = batch * seq
    xf = x.reshape(m, d)
    pf = p.reshape(depth, m, p.shape[-1])
    bf = lambda w: w.astype(_BF16)
    row = lambda g: g.reshape(1, -1)
    pair = lambda g: jnp.concatenate([g, g]).reshape(1, LANES)
    kv_shared = None
    for i in range(depth):
        if i < n_a:
            bias = jnp.repeat(b_spatial[i].T, d // b_spatial.shape[1], axis=1)
            y = _sgu_call(xf, row(ln_mix_a[i]), bf(w_in_a[i]), row(g_v_a[i]), w_spatial[i], bias)
            w_o = w_out_a[i]
        else:
            j = i - n_a
            q = _proj_call(xf, row(ln_mix_b[j]), bf(w_q[j]), pair(g_q[j]),
                           d // LANES, HEAD_DIM ** -0.5)
            y = _attn_call(q, kv_shared, batch, seq)
            w_o = w_out_b[j]
        xf = _post_call(xf, y, pf[i], bf(w_o), row(ln_mlp[i]), bf(w_up[i]), bf(w_down[i]),
                        row(ln_ple[i]), bf(w_ple_gate[i]), bf(w_ple_proj[i]))
        if i == n_a - 1:
            kv_shared = _proj_call(xf, row(ln_kv), bf(w_kv), pair(g_k), d // LANES, 1.0)
    return xf.reshape(batch, seq, d)
```

```python
import functools

import jax
import jax.numpy as jnp
from jax import lax
from jax.experimental import pallas as pl
from jax.experimental.pallas import tpu as pltpu

EPS = 1e-6
CHUNK = 128
HEAD_DIM = 64
LANES = 128
TOKEN_TILE = 256
ATTN_BLOCK = 128
FF_CHUNK = 1024
VMEM_LIMIT = 48 * 1024 * 1024
SUBLANES = 8
KEY_SEGMENT = ATTN_BLOCK // SUBLANES
LOG2_E = 1.4426950408889634
LOG2_WEIGHT_FLOOR = 127.0

_BF16 = jnp.bfloat16
_F32 = jnp.float32


def _rms(x, g):
    return x * lax.rsqrt(jnp.mean(x * x, axis=-1, keepdims=True) + EPS) * g


def _const_spec(shape):
    return pl.BlockSpec(shape, lambda *_: (0,) * len(shape), pipeline_mode=pl.Buffered(1))


def _params(n_axes):
    return pltpu.CompilerParams(
        dimension_semantics=("arbitrary",) * n_axes, vmem_limit_bytes=VMEM_LIMIT)


def _sgu_kernel(x_ref, ln_ref, win_ref, gv_ref, ws_ref, bias_ref, y_ref):
    tm, d = x_ref.shape
    n_groups = ws_ref.shape[0]
    n_chunks = tm // CHUNK
    h = _rms(x_ref[...], ln_ref[...]).astype(_BF16)
    z = jax.nn.gelu(jnp.dot(h, win_ref[...], preferred_element_type=_F32))
    u = z[:, :d]
    v = _rms(z[:, d:], gv_ref[...]).astype(_BF16)
    row = lax.broadcasted_iota(jnp.int32, (CHUNK, CHUNK), 0)
    col = lax.broadcasted_iota(jnp.int32, (CHUNK, CHUNK), 1)
    causal = col <= row
    mix_cols = []
    for g in range(n_groups):
        w = jnp.where(causal, ws_ref[g], 0.0).astype(_BF16)
        vg = jnp.concatenate(
            [v[c * CHUNK:(c + 1) * CHUNK, g * LANES:(g + 1) * LANES] for c in range(n_chunks)],
            axis=1)
        mg = jnp.dot(w, vg, preferred_element_type=_F32)
        mix_cols.append(jnp.concatenate(
            [mg[:, c * LANES:(c + 1) * LANES] for c in range(n_chunks)], axis=0))
    mix = jnp.concatenate(mix_cols, axis=1)
    bias = jnp.concatenate([bias_ref[...]] * n_chunks, axis=0)
    y_ref[...] = (u * (mix + bias)).astype(y_ref.dtype)


def _sgu_call(x, ln, w_in, g_v, w_s, bias):
    m, d = x.shape
    tm = TOKEN_TILE
    return pl.pallas_call(
        _sgu_kernel,
        grid=(m // tm,),
        in_specs=[
            pl.BlockSpec((tm, d), lambda i: (i, 0)),
            _const_spec((1, d)),
            _const_spec(w_in.shape),
            _const_spec((1, d)),
            _const_spec(w_s.shape),
            _const_spec(bias.shape),
        ],
        out_specs=pl.BlockSpec((tm, d), lambda i: (i, 0)),
        out_shape=jax.ShapeDtypeStruct((m, d), _BF16),
        compiler_params=_params(1),
        name="sgu_mixer",
    )(x, ln, w_in, g_v, w_s, bias)


def _post_kernel(x_ref, y_ref, p_ref, wo_ref, lnm_ref, wup_ref, wdn_ref,
                 lnp_ref, wg_ref, wp_ref, o_ref):
    d_ff = wup_ref.shape[1]
    x = x_ref[...] + jnp.dot(y_ref[...], wo_ref[...], preferred_element_type=_F32)
    h = _rms(x, lnm_ref[...]).astype(_BF16)
    acc = x
    for c in range(d_ff // FF_CHUNK):
        a = jnp.dot(h, wup_ref[:, c * FF_CHUNK:(c + 1) * FF_CHUNK], preferred_element_type=_F32)
        a = jnp.maximum(a, 0.0)
        a = (a * a).astype(_BF16)
        acc = acc + jnp.dot(a, wdn_ref[c * FF_CHUNK:(c + 1) * FF_CHUNK, :],
                            preferred_element_type=_F32)
    x = acc
    hg = _rms(x, lnp_ref[...]).astype(_BF16)
    gate = 1.0 / (1.0 + jnp.exp(-jnp.dot(hg, wg_ref[...], preferred_element_type=_F32)))
    pp = jnp.dot(p_ref[...].astype(_BF16), wp_ref[...], preferred_element_type=_F32)
    o_ref[...] = x + pp * gate


def _post_call(x, y, p, w_out, ln_mlp, w_up, w_down, ln_ple, w_gate, w_proj):
    m, d = x.shape
    tm = TOKEN_TILE
    return pl.pallas_call(
        _post_kernel,
        grid=(m // tm,),
        in_specs=[
            pl.BlockSpec((tm, d), lambda i: (i, 0)),
            pl.BlockSpec((tm, d), lambda i: (i, 0)),
            pl.BlockSpec((tm, p.shape[1]), lambda i: (i, 0)),
            _const_spec(w_out.shape),
            _const_spec((1, d)),
            _const_spec(w_up.shape),
            _const_spec(w_down.shape),
            _const_spec((1, d)),
            _const_spec(w_gate.shape),
            _const_spec(w_proj.shape),
        ],
        out_specs=pl.BlockSpec((tm, d), lambda i: (i, 0)),
        out_shape=jax.ShapeDtypeStruct((m, d), _F32),
        compiler_params=_params(1),
        name="post_mlp_ple",
    )(x, y, p, w_out, ln_mlp, w_up, w_down, ln_ple, w_gate, w_proj)


def _proj_kernel(x_ref, ln_ref, w_ref, g_ref, o_ref, *, n_norm, out_scale, segment_major):
    tm = x_ref.shape[0]
    h = _rms(x_ref[...], ln_ref[...]).astype(_BF16)
    if segment_major:
        r = lax.broadcasted_iota(jnp.int32, (tm, tm), 0)
        c = lax.broadcasted_iota(jnp.int32, (tm, tm), 1)
        in_blk = r & (ATTN_BLOCK - 1)
        src = (r - in_blk) + KEY_SEGMENT * (in_blk & (SUBLANES - 1)) + in_blk // SUBLANES
        perm = jnp.where(c == src, 1.0, 0.0).astype(_BF16)
        h = jnp.dot(perm, h, preferred_element_type=_F32).astype(_BF16)
    y = jnp.dot(h, w_ref[...], preferred_element_type=_F32)
    first_head = lax.broadcasted_iota(jnp.int32, (tm, LANES), 1) < HEAD_DIM
    for t in range(y.shape[1] // LANES):
        blk = y[:, t * LANES:(t + 1) * LANES]
        if t < n_norm:
            sq = blk * blk
            s_a = jnp.sum(jnp.where(first_head, sq, 0.0), axis=-1, keepdims=True)
            s_b = jnp.sum(jnp.where(first_head, 0.0, sq), axis=-1, keepdims=True)
            ms = jnp.where(first_head, s_a, s_b) * (1.0 / HEAD_DIM)
            blk = blk * lax.rsqrt(ms + EPS) * g_ref[...] * out_scale
        o_ref[:, t * LANES:(t + 1) * LANES] = blk.astype(o_ref.dtype)


def _proj_call(x, ln, w, g_pair, n_norm, out_scale, segment_major=False):
    m, d = x.shape
    n = w.shape[1]
    tm = TOKEN_TILE
    return pl.pallas_call(
        functools.partial(_proj_kernel, n_norm=n_norm, out_scale=out_scale,
                          segment_major=segment_major),
        grid=(m // tm,),
        in_specs=[
            pl.BlockSpec((tm, d), lambda i: (i, 0)),
            _const_spec((1, d)),
            _const_spec(w.shape),
            _const_spec((1, LANES)),
        ],
        out_specs=pl.BlockSpec((tm, n), lambda i: (i, 0)),
        out_shape=jax.ShapeDtypeStruct((m, n), _BF16),
        compiler_params=_params(1),
        name="head_proj",
    )(x, ln, w, g_pair)


def _attn_kernel(q_ref, k_ref, v_ref, o_ref, qs_ref, swept_ref, acc_ref):
    blk = ATTN_BLOCK
    n_pairs = qs_ref.shape[0]
    n_groups = blk // SUBLANES
    i = pl.program_id(1)
    lane = lax.broadcasted_iota(jnp.int32, (blk, LANES), 1)
    first_head = lane < HEAD_DIM
    for p in range(n_pairs):
        q2 = q_ref[:, p * LANES:(p + 1) * LANES]
        zeros = jnp.zeros_like(q2)
        qs_ref[p] = jnp.concatenate(
            [jnp.where(first_head, q2, zeros), jnp.where(first_head, zeros, q2)], axis=0)
    swept_ref[...] = jnp.zeros_like(swept_ref)
    acc_ref[...] = jnp.zeros_like(acc_ref)
    sub = lax.broadcasted_iota(jnp.int32, (SUBLANES, 2 * blk), 0)

    def sweep_block(j, diagonal):
        start = pl.multiple_of(j * blk, blk)
        if diagonal:
            r = lax.broadcasted_iota(jnp.int32, (blk, 2 * blk), 0)
            key = KEY_SEGMENT * (r & (SUBLANES - 1)) + r // SUBLANES
            query = lax.broadcasted_iota(jnp.int32, (blk, 2 * blk), 1) & (blk - 1)
            valid = key < query
        pairs = range(n_pairs)
        zs = [lax.dot_general(k_ref[pl.ds(start, blk), p * LANES:(p + 1) * LANES], qs_ref[p],
                              (((1,), (1,)), ((), ())), preferred_element_type=_F32)
              for p in pairs]
        weights = []
        for p in pairs:
            z = zs[p]
            sp = jnp.maximum(z, 0.0) + jnp.log(1.0 + jnp.exp2(-jnp.abs(z))) * LOG2_E
            if diagonal:
                sp = jnp.where(valid, sp, 0.0)
            runs = [None] * n_groups
            run = sp[(n_groups - 1) * SUBLANES:]
            runs[n_groups - 1] = run
            for v in range(n_groups - 2, -1, -1):
                run = run + sp[v * SUBLANES:(v + 1) * SUBLANES]
                runs[v] = run
            suffix = run
            for step in (1, 2, 4):
                shifted = pltpu.roll(suffix, SUBLANES - step, 0)
                suffix = suffix + jnp.where(sub + step < SUBLANES, shifted, 0.0)
            swept = swept_ref[p]
            base = (suffix - run) + swept
            swept_ref[p] = swept + jnp.broadcast_to(suffix[0:1], swept.shape)
            w = jnp.concatenate(
                [jnp.exp2(z[v * SUBLANES:(v + 1) * SUBLANES] - (runs[v] + base))
                 for v in range(n_groups)], axis=0)
            if diagonal:
                w = jnp.where(valid, w, 0.0)
            weights.append(w.astype(_BF16))
        low = None
        for p in pairs:
            acc_ref[p] += lax.dot_general(
                v_ref[pl.ds(start, blk), p * LANES:(p + 1) * LANES], weights[p],
                (((0,), (0,)), ((), ())), preferred_element_type=_F32)
            low = swept_ref[p] if low is None else jnp.minimum(low, swept_ref[p])
        return jnp.min(low)

    low0 = sweep_block(i, diagonal=True)

    def cond(carry):
        j, low = carry
        return jnp.logical_and(j >= 0, low < LOG2_WEIGHT_FLOOR)

    def body(carry):
        j, _ = carry
        return j - 1, sweep_block(j, diagonal=False)

    lax.while_loop(cond, body, (i - 1, low0))
    for p in range(n_pairs):
        acc = acc_ref[p]
        out_t = jnp.concatenate([acc[:HEAD_DIM, :blk], acc[HEAD_DIM:, blk:]], axis=0)
        o_ref[:, p * LANES:(p + 1) * LANES] = out_t.T.astype(o_ref.dtype)


def _attn_call(q, kv, batch, seq):
    m, d = q.shape
    blk = ATTN_BLOCK
    n_q = seq // blk
    n_pairs = d // LANES
    return pl.pallas_call(
        _attn_kernel,
        grid=(batch, n_q),
        in_specs=[
            pl.BlockSpec((blk, d), lambda b, i: (b * n_q + i, 0)),
            pl.BlockSpec((seq, d), lambda b, i: (b, 0)),
            pl.BlockSpec((seq, d), lambda b, i: (b, 1)),
        ],
        out_specs=pl.BlockSpec((blk, d), lambda b, i: (b * n_q + i, 0)),
        out_shape=jax.ShapeDtypeStruct((m, d), _BF16),
        scratch_shapes=[
            pltpu.VMEM((n_pairs, 2 * blk, LANES), _BF16),
            pltpu.VMEM((n_pairs, SUBLANES, 2 * blk), _F32),
            pltpu.VMEM((n_pairs, LANES, 2 * blk), _F32),
        ],
        compiler_params=_params(2),
        name="stick_breaking",
    )(q, kv, kv)


def kernel(x, p, ln_mix_a, w_in_a, g_v_a, w_spatial, b_spatial, w_out_a, ln_kv, w_kv, g_k,
           ln_mix_b, w_q, g_q, w_out_b, ln_mlp, w_up, w_down, ln_ple, w_ple_gate, w_ple_proj):
    batch, seq, d = x.shape
    depth = p.shape[0]
    n_a = ln_mix_a.shape[0]
    m = batch * seq
    xf = x.reshape(m, d)
    pf = p.reshape(depth, m, p.shape[-1])
    bf = lambda w: w.astype(_BF16)
    row = lambda g: g.reshape(1, -1)
    pair = lambda g: jnp.concatenate([g, g]).reshape(1, LANES)
    kv_shared = None
    for i in range(depth):
        if i < n_a:
            bias = jnp.repeat(b_spatial[i].T, d // b_spatial.shape[1], axis=1)
            y = _sgu_call(xf, row(ln_mix_a[i]), bf(w_in_a[i]), row(g_v_a[i]), w_spatial[i], bias)
            w_o = w_out_a[i]
        else:
            j = i - n_a
            q = _proj_call(xf, row(ln_mix_b[j]), bf(w_q[j]), pair(g_q[j]),
                           d // LANES, LOG2_E * HEAD_DIM ** -0.5)
            y = _attn_call(q, kv_shared, batch, seq)
            w_o = w_out_b[j]
        xf = _post_call(xf, y, pf[i], bf(w_o), row(ln_mlp[i]), bf(w_up[i]), bf(w_down[i]),
                        row(ln_ple[i]), bf(w_ple_gate[i]), bf(w_ple_proj[i]))
        if i == n_a - 1:
            kv_shared = _proj_call(xf, row(ln_kv), bf(w_kv), pair(g_k), d // LANES, 1.0,
                                   segment_major=True)
    return xf.reshape(batch, seq, d)
```

```python
import functools

import jax
import jax.numpy as jnp
from jax import lax
from jax.experimental import pallas as pl
from jax.experimental.pallas import tpu as pltpu

EPS = 1e-6
CHUNK = 128
HEAD_DIM = 64
LANES = 128
TOKEN_TILE = 512
ATTN_BLOCK = 128
FF_CHUNK = 1024
VMEM_LIMIT = 48 * 1024 * 1024
SUBLANES = 8
KEY_SEGMENT = ATTN_BLOCK // SUBLANES
LOG2_E = 1.4426950408889634
LOG2_WEIGHT_FLOOR = 127.0

_BF16 = jnp.bfloat16
_F32 = jnp.float32


def _rms(x, g):
    return x * lax.rsqrt(jnp.mean(x * x, axis=-1, keepdims=True) + EPS) * g


def _gelu_tanh(x):
    c = -2.0 * LOG2_E * (2.0 / jnp.pi) ** 0.5
    return x / (1.0 + jnp.exp2(x * (x * x * (c * 0.044715) + c)))


def _const_spec(shape):
    return pl.BlockSpec(shape, lambda *_: (0,) * len(shape), pipeline_mode=pl.Buffered(1))


def _params(n_axes):
    return pltpu.CompilerParams(
        dimension_semantics=("arbitrary",) * n_axes, vmem_limit_bytes=VMEM_LIMIT)


def _sgu_kernel(x_ref, ln_ref, win_ref, gv_ref, ws_ref, bias_ref, y_ref):
    tm, d = x_ref.shape
    n_groups = ws_ref.shape[0]
    n_chunks = tm // CHUNK
    h = _rms(x_ref[...], ln_ref[...]).astype(_BF16)
    z = _gelu_tanh(jnp.dot(h, win_ref[...], preferred_element_type=_F32))
    u = z[:, :d]
    v = _rms(z[:, d:], gv_ref[...]).astype(_BF16)
    row = lax.broadcasted_iota(jnp.int32, (CHUNK, CHUNK), 0)
    col = lax.broadcasted_iota(jnp.int32, (CHUNK, CHUNK), 1)
    causal = col <= row
    mix_cols = []
    for g in range(n_groups):
        w = jnp.where(causal, ws_ref[g], 0.0).astype(_BF16)
        vg = jnp.concatenate(
            [v[c * CHUNK:(c + 1) * CHUNK, g * LANES:(g + 1) * LANES] for c in range(n_chunks)],
            axis=1)
        mg = jnp.dot(w, vg, preferred_element_type=_F32)
        mix_cols.append(jnp.concatenate(
            [mg[:, c * LANES:(c + 1) * LANES] for c in range(n_chunks)], axis=0))
    mix = jnp.concatenate(mix_cols, axis=1)
    bias = jnp.concatenate([bias_ref[...]] * n_chunks, axis=0)
    y_ref[...] = (u * (mix + bias)).astype(y_ref.dtype)


def _sgu_call(x, ln, w_in, g_v, w_s, bias):
    m, d = x.shape
    tm = TOKEN_TILE
    return pl.pallas_call(
        _sgu_kernel,
        grid=(m // tm,),
        in_specs=[
            pl.BlockSpec((tm, d), lambda i: (i, 0)),
            _const_spec((1, d)),
            _const_spec(w_in.shape),
            _const_spec((1, d)),
            _const_spec(w_s.shape),
            _const_spec(bias.shape),
        ],
        out_specs=pl.BlockSpec((tm, d), lambda i: (i, 0)),
        out_shape=jax.ShapeDtypeStruct((m, d), _BF16),
        compiler_params=_params(1),
        name="sgu_mixer",
    )(x, ln, w_in, g_v, w_s, bias)


def _post_kernel(x_ref, y_ref, p_ref, wo_ref, lnm_ref, wup_ref, wdn_ref,
                 lnp_ref, wg_ref, wp_ref, o_ref):
    d_ff = wup_ref.shape[1]
    x = x_ref[...] + jnp.dot(y_ref[...], wo_ref[...], preferred_element_type=_F32)
    h = _rms(x, lnm_ref[...]).astype(_BF16)
    acc = x
    for c in range(d_ff // FF_CHUNK):
        a = jnp.dot(h, wup_ref[:, c * FF_CHUNK:(c + 1) * FF_CHUNK], preferred_element_type=_F32)
        a = jnp.maximum(a, 0.0)
        a = (a * a).astype(_BF16)
        acc = acc + jnp.dot(a, wdn_ref[c * FF_CHUNK:(c + 1) * FF_CHUNK, :],
                            preferred_element_type=_F32)
    x = acc
    hg = _rms(x, lnp_ref[...]).astype(_BF16)
    gate = 1.0 / (1.0 + jnp.exp(-jnp.dot(hg, wg_ref[...], preferred_element_type=_F32)))
    pp = jnp.dot(p_ref[...].astype(_BF16), wp_ref[...], preferred_element_type=_F32)
    o_ref[...] = x + pp * gate


def _post_call(x, y, p, w_out, ln_mlp, w_up, w_down, ln_ple, w_gate, w_proj):
    m, d = x.shape
    tm = TOKEN_TILE
    return pl.pallas_call(
        _post_kernel,
        grid=(m // tm,),
        in_specs=[
            pl.BlockSpec((tm, d), lambda i: (i, 0)),
            pl.BlockSpec((tm, d), lambda i: (i, 0)),
            pl.BlockSpec((tm, p.shape[1]), lambda i: (i, 0)),
            _const_spec(w_out.shape),
            _const_spec((1, d)),
            _const_spec(w_up.shape),
            _const_spec(w_down.shape),
            _const_spec((1, d)),
            _const_spec(w_gate.shape),
            _const_spec(w_proj.shape),
        ],
        out_specs=pl.BlockSpec((tm, d), lambda i: (i, 0)),
        out_shape=jax.ShapeDtypeStruct((m, d), _F32),
        compiler_params=_params(1),
        name="post_mlp_ple",
    )(x, y, p, w_out, ln_mlp, w_up, w_down, ln_ple, w_gate, w_proj)


def _proj_kernel(x_ref, ln_ref, w_ref, g_ref, o_ref, *, n_norm, out_scale, segment_major):
    tm = x_ref.shape[0]
    h = _rms(x_ref[...], ln_ref[...]).astype(_BF16)
    if segment_major:
        r = lax.broadcasted_iota(jnp.int32, (ATTN_BLOCK, ATTN_BLOCK), 0)
        c = lax.broadcasted_iota(jnp.int32, (ATTN_BLOCK, ATTN_BLOCK), 1)
        src = KEY_SEGMENT * (r & (SUBLANES - 1)) + r // SUBLANES
        perm = jnp.where(c == src, 1.0, 0.0).astype(_BF16)
        h = jnp.concatenate(
            [jnp.dot(perm, h[b * ATTN_BLOCK:(b + 1) * ATTN_BLOCK],
                     preferred_element_type=_F32).astype(_BF16)
             for b in range(tm // ATTN_BLOCK)], axis=0)
    y = jnp.dot(h, w_ref[...], preferred_element_type=_F32)
    first_head = lax.broadcasted_iota(jnp.int32, (tm, LANES), 1) < HEAD_DIM
    for t in range(y.shape[1] // LANES):
        blk = y[:, t * LANES:(t + 1) * LANES]
        if t < n_norm:
            sq = blk * blk
            s_a = jnp.sum(jnp.where(first_head, sq, 0.0), axis=-1, keepdims=True)
            s_b = jnp.sum(jnp.where(first_head, 0.0, sq), axis=-1, keepdims=True)
            ms = jnp.where(first_head, s_a, s_b) * (1.0 / HEAD_DIM)
            blk = blk * lax.rsqrt(ms + EPS) * g_ref[...] * out_scale
        o_ref[:, t * LANES:(t + 1) * LANES] = blk.astype(o_ref.dtype)


def _proj_call(x, ln, w, g_pair, n_norm, out_scale, segment_major=False):
    m, d = x.shape
    n = w.shape[1]
    tm = TOKEN_TILE
    return pl.pallas_call(
        functools.partial(_proj_kernel, n_norm=n_norm, out_scale=out_scale,
                          segment_major=segment_major),
        grid=(m // tm,),
        in_specs=[
            pl.BlockSpec((tm, d), lambda i: (i, 0)),
            _const_spec((1, d)),
            _const_spec(w.shape),
            _const_spec((1, LANES)),
        ],
        out_specs=pl.BlockSpec((tm, n), lambda i: (i, 0)),
        out_shape=jax.ShapeDtypeStruct((m, n), _BF16),
        compiler_params=_params(1),
        name="head_proj",
    )(x, ln, w, g_pair)


def _attn_kernel(q_ref, k_ref, v_ref, o_ref, qs_ref, swept_ref, acc_ref):
    blk = ATTN_BLOCK
    n_pairs = qs_ref.shape[0]
    n_groups = blk // SUBLANES
    lane = lax.broadcasted_iota(jnp.int32, (blk, LANES), 1)
    first_head = lane < HEAD_DIM
    sub = lax.broadcasted_iota(jnp.int32, (SUBLANES, 2 * blk), 0)

    def sweep_block(j, diagonal):
        start = pl.multiple_of(j * blk, blk)
        if diagonal:
            r = lax.broadcasted_iota(jnp.int32, (blk, 2 * blk), 0)
            key = KEY_SEGMENT * (r & (SUBLANES - 1)) + r // SUBLANES
            query = lax.broadcasted_iota(jnp.int32, (blk, 2 * blk), 1) & (blk - 1)
            valid = key < query
        pairs = range(n_pairs)
        zs = [lax.dot_general(k_ref[pl.ds(start, blk), p * LANES:(p + 1) * LANES], qs_ref[p],
                              (((1,), (1,)), ((), ())), preferred_element_type=_F32)
              for p in pairs]
        weights = []
        for p in pairs:
            z = zs[p]
            sp = jnp.maximum(z, 0.0) + jnp.log(1.0 + jnp.exp2(-jnp.abs(z))) * LOG2_E
            if diagonal:
                sp = jnp.where(valid, sp, 0.0)
            runs = [None] * n_groups
            run = sp[(n_groups - 1) * SUBLANES:]
            runs[n_groups - 1] = run
            for v in range(n_groups - 2, -1, -1):
                run = run + sp[v * SUBLANES:(v + 1) * SUBLANES]
                runs[v] = run
            suffix = run
            for step in (1, 2, 4):
                shifted = pltpu.roll(suffix, SUBLANES - step, 0)
                suffix = suffix + jnp.where(sub + step < SUBLANES, shifted, 0.0)
            swept = swept_ref[p]
            base = (suffix - run) + swept
            swept_ref[p] = swept + jnp.broadcast_to(suffix[0:1], swept.shape)
            w = jnp.concatenate(
                [jnp.exp2(z[v * SUBLANES:(v + 1) * SUBLANES] - (runs[v] + base))
                 for v in range(n_groups)], axis=0)
            if diagonal:
                w = jnp.where(valid, w, 0.0)
            weights.append(w.astype(_BF16))
        for p in pairs:
            acc_ref[p] += lax.dot_general(
                v_ref[pl.ds(start, blk), p * LANES:(p + 1) * LANES], weights[p],
                (((0,), (0,)), ((), ())), preferred_element_type=_F32)

    def least_swept():
        low = swept_ref[0]
        for p in range(1, n_pairs):
            low = jnp.minimum(low, swept_ref[p])
        return jnp.min(low)

    def query_block(i, carry):
        rows = pl.ds(pl.multiple_of(i * blk, blk), blk)
        for p in range(n_pairs):
            q2 = q_ref[rows, p * LANES:(p + 1) * LANES]
            zeros = jnp.zeros_like(q2)
            qs_ref[p] = jnp.concatenate(
                [jnp.where(first_head, q2, zeros), jnp.where(first_head, zeros, q2)], axis=0)
        swept_ref[...] = jnp.zeros_like(swept_ref)
        acc_ref[...] = jnp.zeros_like(acc_ref)

        sweep_block(i, diagonal=True)

        @pl.when(i > 0)
        def _():
            sweep_block(i - 1, diagonal=False)

        def cond(state):
            j, low = state
            return jnp.logical_and(j >= 0, low < LOG2_WEIGHT_FLOOR)

        def body(state):
            j, _ = state
            sweep_block(j, diagonal=False)
            return j - 1, least_swept()

        lax.while_loop(cond, body, (i - 2, least_swept()))
        for p in range(n_pairs):
            acc = acc_ref[p]
            out_t = jnp.concatenate([acc[:HEAD_DIM, :blk], acc[HEAD_DIM:, blk:]], axis=0)
            o_ref[rows, p * LANES:(p + 1) * LANES] = out_t.T.astype(o_ref.dtype)
        return carry

    lax.fori_loop(0, q_ref.shape[0] // blk, query_block, 0)


def _attn_call(q, kv, batch, seq):
    m, d = q.shape
    blk = ATTN_BLOCK
    n_pairs = d // LANES
    return pl.pallas_call(
        _attn_kernel,
        grid=(batch,),
        in_specs=[
            pl.BlockSpec((seq, d), lambda b: (b, 0)),
            pl.BlockSpec((seq, d), lambda b: (b, 0)),
            pl.BlockSpec((seq, d), lambda b: (b, 1)),
        ],
        out_specs=pl.BlockSpec((seq, d), lambda b: (b, 0)),
        out_shape=jax.ShapeDtypeStruct((m, d), _BF16),
        scratch_shapes=[
            pltpu.VMEM((n_pairs, 2 * blk, LANES), _BF16),
            pltpu.VMEM((n_pairs, SUBLANES, 2 * blk), _F32),
            pltpu.VMEM((n_pairs, LANES, 2 * blk), _F32),
        ],
        compiler_params=_params(1),
        name="stick_breaking",
    )(q, kv, kv)


def kernel(x, p, ln_mix_a, w_in_a, g_v_a, w_spatial, b_spatial, w_out_a, ln_kv, w_kv, g_k,
           ln_mix_b, w_q, g_q, w_out_b, ln_mlp, w_up, w_down, ln_ple, w_ple_gate, w_ple_proj):
    batch, seq, d = x.shape
    depth = p.shape[0]
    n_a = ln_mix_a.shape[0]
    m = batch * seq
    xf = x.reshape(m, d)
    pf = p.reshape(depth, m, p.shape[-1])
    bf = lambda w: w.astype(_BF16)
    row = lambda g: g.reshape(1, -1)
    pair = lambda g: jnp.concatenate([g, g]).reshape(1, LANES)
    kv_shared = None
    for i in range(depth):
        if i < n_a:
            bias = jnp.repeat(b_spatial[i].T, d // b_spatial.shape[1], axis=1)
            y = _sgu_call(xf, row(ln_mix_a[i]), bf(w_in_a[i]), row(g_v_a[i]), w_spatial[i], bias)
            w_o = w_out_a[i]
        else:
            j = i - n_a
            q = _proj_call(xf, row(ln_mix_b[j]), bf(w_q[j]), pair(g_q[j]),
                           d // LANES, LOG2_E * HEAD_DIM ** -0.5)
            y = _attn_call(q, kv_shared, batch, seq)
            w_o = w_out_b[j]
        xf = _post_call(xf, y, pf[i], bf(w_o), row(ln_mlp[i]), bf(w_up[i]), bf(w_down[i]),
                        row(ln_ple[i]), bf(w_ple_gate[i]), bf(w_ple_proj[i]))
        if i == n_a - 1:
            kv_shared = _proj_call(xf, row(ln_kv), bf(w_kv), pair(g_k), d // LANES, 1.0,
                                   segment_major=True)
    return xf.reshape(batch, seq, d)
```

```python
import functools

import jax
import jax.numpy as jnp
from jax import lax
from jax.experimental import pallas as pl
from jax.experimental.pallas import tpu as pltpu

EPS = 1e-6
CHUNK = 128
HEAD_DIM = 64
LANES = 128
TOKEN_TILE = 512
ATTN_BLOCK = 128
STATIC_SWEEP = 3
FF_CHUNK = 1024
VMEM_LIMIT = 48 * 1024 * 1024
SUBLANES = 8
KEY_SEGMENT = ATTN_BLOCK // SUBLANES
LOG2_E = 1.4426950408889634
LOG2_WEIGHT_FLOOR = 127.0

_BF16 = jnp.bfloat16
_F32 = jnp.float32


def _rms(x, g):
    return x * lax.rsqrt(jnp.mean(x * x, axis=-1, keepdims=True) + EPS) * g


def _gelu_tanh(x):
    c = -2.0 * LOG2_E * (2.0 / jnp.pi) ** 0.5
    return x / (1.0 + jnp.exp2(x * (x * x * (c * 0.044715) + c)))


def _const_spec(shape):
    return pl.BlockSpec(shape, lambda *_: (0,) * len(shape), pipeline_mode=pl.Buffered(1))


def _params(n_axes):
    return pltpu.CompilerParams(
        dimension_semantics=("arbitrary",) * n_axes, vmem_limit_bytes=VMEM_LIMIT)


def _sgu_kernel(x_ref, ln_ref, win_ref, gv_ref, ws_ref, bias_ref, y_ref):
    tm, d = x_ref.shape
    n_groups = ws_ref.shape[0]
    n_chunks = tm // CHUNK
    h = _rms(x_ref[...], ln_ref[...]).astype(_BF16)
    z = _gelu_tanh(jnp.dot(h, win_ref[...], preferred_element_type=_F32))
    u = z[:, :d]
    v = _rms(z[:, d:], gv_ref[...]).astype(_BF16)
    row = lax.broadcasted_iota(jnp.int32, (CHUNK, CHUNK), 0)
    col = lax.broadcasted_iota(jnp.int32, (CHUNK, CHUNK), 1)
    causal = col <= row
    mix_cols = []
    for g in range(n_groups):
        w = jnp.where(causal, ws_ref[g], 0.0).astype(_BF16)
        vg = jnp.concatenate(
            [v[c * CHUNK:(c + 1) * CHUNK, g * LANES:(g + 1) * LANES] for c in range(n_chunks)],
            axis=1)
        mg = jnp.dot(w, vg, preferred_element_type=_F32)
        mix_cols.append(jnp.concatenate(
            [mg[:, c * LANES:(c + 1) * LANES] for c in range(n_chunks)], axis=0))
    mix = jnp.concatenate(mix_cols, axis=1)
    bias = jnp.concatenate([bias_ref[...]] * n_chunks, axis=0)
    y_ref[...] = (u * (mix + bias)).astype(y_ref.dtype)


def _sgu_call(x, ln, w_in, g_v, w_s, bias):
    m, d = x.shape
    tm = TOKEN_TILE
    return pl.pallas_call(
        _sgu_kernel,
        grid=(m // tm,),
        in_specs=[
            pl.BlockSpec((tm, d), lambda i: (i, 0)),
            _const_spec((1, d)),
            _const_spec(w_in.shape),
            _const_spec((1, d)),
            _const_spec(w_s.shape),
            _const_spec(bias.shape),
        ],
        out_specs=pl.BlockSpec((tm, d), lambda i: (i, 0)),
        out_shape=jax.ShapeDtypeStruct((m, d), _BF16),
        compiler_params=_params(1),
        name="sgu_mixer",
    )(x, ln, w_in, g_v, w_s, bias)


def _post_kernel(x_ref, y_ref, p_ref, wo_ref, lnm_ref, wup_ref, wdn_ref,
                 lnp_ref, wg_ref, wp_ref, o_ref):
    d_ff = wup_ref.shape[1]
    x = x_ref[...] + jnp.dot(y_ref[...], wo_ref[...], preferred_element_type=_F32)
    h = _rms(x, lnm_ref[...]).astype(_BF16)
    acc = x
    for c in range(d_ff // FF_CHUNK):
        a = jnp.dot(h, wup_ref[:, c * FF_CHUNK:(c + 1) * FF_CHUNK], preferred_element_type=_F32)
        a = jnp.maximum(a, 0.0)
        a = (a * a).astype(_BF16)
        acc = acc + jnp.dot(a, wdn_ref[c * FF_CHUNK:(c + 1) * FF_CHUNK, :],
                            preferred_element_type=_F32)
    x = acc
    hg = _rms(x, lnp_ref[...]).astype(_BF16)
    gate = 1.0 / (1.0 + jnp.exp(-jnp.dot(hg, wg_ref[...], preferred_element_type=_F32)))
    pp = jnp.dot(p_ref[...].astype(_BF16), wp_ref[...], preferred_element_type=_F32)
    o_ref[...] = x + pp * gate


def _post_call(x, y, p, w_out, ln_mlp, w_up, w_down, ln_ple, w_gate, w_proj):
    m, d = x.shape
    tm = TOKEN_TILE
    return pl.pallas_call(
        _post_kernel,
        grid=(m // tm,),
        in_specs=[
            pl.BlockSpec((tm, d), lambda i: (i, 0)),
            pl.BlockSpec((tm, d), lambda i: (i, 0)),
            pl.BlockSpec((tm, p.shape[1]), lambda i: (i, 0)),
            _const_spec(w_out.shape),
            _const_spec((1, d)),
            _const_spec(w_up.shape),
            _const_spec(w_down.shape),
            _const_spec((1, d)),
            _const_spec(w_gate.shape),
            _const_spec(w_proj.shape),
        ],
        out_specs=pl.BlockSpec((tm, d), lambda i: (i, 0)),
        out_shape=jax.ShapeDtypeStruct((m, d), _F32),
        compiler_params=_params(1),
        name="post_mlp_ple",
    )(x, y, p, w_out, ln_mlp, w_up, w_down, ln_ple, w_gate, w_proj)


def _proj_kernel(x_ref, ln_ref, w_ref, g_ref, o_ref, *, n_norm, out_scale, segment_major):
    tm = x_ref.shape[0]
    h = _rms(x_ref[...], ln_ref[...]).astype(_BF16)
    if segment_major:
        r = lax.broadcasted_iota(jnp.int32, (ATTN_BLOCK, ATTN_BLOCK), 0)
        c = lax.broadcasted_iota(jnp.int32, (ATTN_BLOCK, ATTN_BLOCK), 1)
        src = KEY_SEGMENT * (r & (SUBLANES - 1)) + r // SUBLANES
        perm = jnp.where(c == src, 1.0, 0.0).astype(_BF16)
        h = jnp.concatenate(
            [jnp.dot(perm, h[b * ATTN_BLOCK:(b + 1) * ATTN_BLOCK],
                     preferred_element_type=_F32).astype(_BF16)
             for b in range(tm // ATTN_BLOCK)], axis=0)
    y = jnp.dot(h, w_ref[...], preferred_element_type=_F32)
    first_head = lax.broadcasted_iota(jnp.int32, (tm, LANES), 1) < HEAD_DIM
    for t in range(y.shape[1] // LANES):
        blk = y[:, t * LANES:(t + 1) * LANES]
        if t < n_norm:
            sq = blk * blk
            s_a = jnp.sum(jnp.where(first_head, sq, 0.0), axis=-1, keepdims=True)
            s_b = jnp.sum(jnp.where(first_head, 0.0, sq), axis=-1, keepdims=True)
            ms = jnp.where(first_head, s_a, s_b) * (1.0 / HEAD_DIM)
            blk = blk * lax.rsqrt(ms + EPS) * g_ref[...] * out_scale
        o_ref[:, t * LANES:(t + 1) * LANES] = blk.astype(o_ref.dtype)


def _proj_call(x, ln, w, g_pair, n_norm, out_scale, segment_major=False):
    m, d = x.shape
    n = w.shape[1]
    tm = TOKEN_TILE
    return pl.pallas_call(
        functools.partial(_proj_kernel, n_norm=n_norm, out_scale=out_scale,
                          segment_major=segment_major),
        grid=(m // tm,),
        in_specs=[
            pl.BlockSpec((tm, d), lambda i: (i, 0)),
            _const_spec((1, d)),
            _const_spec(w.shape),
            _const_spec((1, LANES)),
        ],
        out_specs=pl.BlockSpec((tm, n), lambda i: (i, 0)),
        out_shape=jax.ShapeDtypeStruct((m, n), _BF16),
        compiler_params=_params(1),
        name="head_proj",
    )(x, ln, w, g_pair)


def _attn_kernel(q_ref, k_ref, v_ref, o_ref, qs_ref, swept_ref, acc_ref):
    blk = ATTN_BLOCK
    n_pairs = qs_ref.shape[0]
    n_groups = blk // SUBLANES
    lane = lax.broadcasted_iota(jnp.int32, (blk, LANES), 1)
    first_head = lane < HEAD_DIM
    sub = lax.broadcasted_iota(jnp.int32, (SUBLANES, 2 * blk), 0)

    def block_weights(z, swept, valid):
        sp = jnp.maximum(z, 0.0) + jnp.log(1.0 + jnp.exp2(-jnp.abs(z))) * LOG2_E
        if valid is not None:
            sp = jnp.where(valid, sp, 0.0)
        runs = [None] * n_groups
        run = sp[(n_groups - 1) * SUBLANES:]
        runs[n_groups - 1] = run
        for v in range(n_groups - 2, -1, -1):
            run = run + sp[v * SUBLANES:(v + 1) * SUBLANES]
            runs[v] = run
        suffix = run
        for step in (1, 2, 4):
            shifted = pltpu.roll(suffix, SUBLANES - step, 0)
            suffix = suffix + jnp.where(sub + step < SUBLANES, shifted, 0.0)
        base = suffix - run
        if swept is not None:
            base = base + swept
        w = jnp.concatenate(
            [jnp.exp2(z[v * SUBLANES:(v + 1) * SUBLANES] - (runs[v] + base))
             for v in range(n_groups)], axis=0)
        if valid is not None:
            w = jnp.where(valid, w, 0.0)
        total = jnp.broadcast_to(suffix[0:1], suffix.shape)
        return w.astype(_BF16), total if swept is None else swept + total

    def sweep(j_last, n_blocks, first):
        rows = pl.ds(pl.multiple_of((j_last - (n_blocks - 1)) * blk, blk), n_blocks * blk)
        valid = None
        if first:
            r = lax.broadcasted_iota(jnp.int32, (blk, 2 * blk), 0)
            key = KEY_SEGMENT * (r & (SUBLANES - 1)) + r // SUBLANES
            query = lax.broadcasted_iota(jnp.int32, (blk, 2 * blk), 1) & (blk - 1)
            valid = key < query
        pairs = range(n_pairs)
        zs = [lax.dot_general(k_ref[rows, p * LANES:(p + 1) * LANES], qs_ref[p],
                              (((1,), (1,)), ((), ())), preferred_element_type=_F32)
              for p in pairs]
        weights = []
        for p in pairs:
            swept = None if first else swept_ref[p]
            parts = [None] * n_blocks
            for b in range(n_blocks - 1, -1, -1):
                parts[b], swept = block_weights(
                    zs[p][b * blk:(b + 1) * blk], swept,
                    valid if first and b == n_blocks - 1 else None)
            swept_ref[p] = swept
            weights.append(jnp.concatenate(parts, axis=0) if n_blocks > 1 else parts[0])
        for p in pairs:
            acc = lax.dot_general(v_ref[rows, p * LANES:(p + 1) * LANES], weights[p],
                                  (((0,), (0,)), ((), ())), preferred_element_type=_F32)
            acc_ref[p] = acc if first else acc_ref[p] + acc

    def least_swept():
        low = swept_ref[0]
        for p in range(1, n_pairs):
            low = jnp.minimum(low, swept_ref[p])
        return jnp.min(low)

    def query_block(i, carry):
        rows = pl.ds(pl.multiple_of(i * blk, blk), blk)
        for p in range(n_pairs):
            q2 = q_ref[rows, p * LANES:(p + 1) * LANES]
            zeros = jnp.zeros_like(q2)
            qs_ref[p] = jnp.concatenate(
                [jnp.where(first_head, q2, zeros), jnp.where(first_head, zeros, q2)], axis=0)

        for n in range(1, STATIC_SWEEP + 1):
            @pl.when((i == n - 1) if n < STATIC_SWEEP else (i >= n - 1))
            def _(n=n):
                sweep(i, n, first=True)

        def cond(state):
            j, low = state
            return jnp.logical_and(j >= 0, low < LOG2_WEIGHT_FLOOR)

        def body(state):
            j, _ = state
            sweep(j, 1, first=False)
            return j - 1, least_swept()

        lax.while_loop(cond, body, (i - STATIC_SWEEP, least_swept()))
        for p in range(n_pairs):
            acc = acc_ref[p]
            out_t = jnp.concatenate([acc[:HEAD_DIM, :blk], acc[HEAD_DIM:, blk:]], axis=0)
            o_ref[rows, p * LANES:(p + 1) * LANES] = out_t.T.astype(o_ref.dtype)
        return carry

    lax.fori_loop(0, q_ref.shape[0] // blk, query_block, 0)


def _attn_call(q, kv, batch, seq):
    m, d = q.shape
    blk = ATTN_BLOCK
    n_pairs = d // LANES
    return pl.pallas_call(
        _attn_kernel,
        grid=(batch,),
        in_specs=[
            pl.BlockSpec((seq, d), lambda b: (b, 0)),
            pl.BlockSpec((seq, d), lambda b: (b, 0)),
            pl.BlockSpec((seq, d), lambda b: (b, 1)),
        ],
        out_specs=pl.BlockSpec((seq, d), lambda b: (b, 0)),
        out_shape=jax.ShapeDtypeStruct((m, d), _BF16),
        scratch_shapes=[
            pltpu.VMEM((n_pairs, 2 * blk, LANES), _BF16),
            pltpu.VMEM((n_pairs, SUBLANES, 2 * blk), _F32),
            pltpu.VMEM((n_pairs, LANES, 2 * blk), _F32),
        ],
        compiler_params=_params(1),
        name="stick_breaking",
    )(q, kv, kv)


def kernel(x, p, ln_mix_a, w_in_a, g_v_a, w_spatial, b_spatial, w_out_a, ln_kv, w_kv, g_k,
           ln_mix_b, w_q, g_q, w_out_b, ln_mlp, w_up, w_down, ln_ple, w_ple_gate, w_ple_proj):
    batch, seq, d = x.shape
    depth = p.shape[0]
    n_a = ln_mix_a.shape[0]
    m = batch * seq
    xf = x.reshape(m, d)
    pf = p.reshape(depth, m, p.shape[-1])
    bf = lambda w: w.astype(_BF16)
    row = lambda g: g.reshape(1, -1)
    pair = lambda g: jnp.concatenate([g, g]).reshape(1, LANES)
    kv_shared = None
    for i in range(depth):
        if i < n_a:
            bias = jnp.repeat(b_spatial[i].T, d // b_spatial.shape[1], axis=1)
            y = _sgu_call(xf, row(ln_mix_a[i]), bf(w_in_a[i]), row(g_v_a[i]), w_spatial[i], bias)
            w_o = w_out_a[i]
        else:
            j = i - n_a
            q = _proj_call(xf, row(ln_mix_b[j]), bf(w_q[j]), pair(g_q[j]),
                           d // LANES, LOG2_E * HEAD_DIM ** -0.5)
            y = _attn_call(q, kv_shared, batch, seq)
            w_o = w_out_b[j]
        xf = _post_call(xf, y, pf[i], bf(w_o), row(ln_mlp[i]), bf(w_up[i]), bf(w_down[i]),
                        row(ln_ple[i]), bf(w_ple_gate[i]), bf(w_ple_proj[i]))
        if i == n_a - 1:
            kv_shared = _proj_call(xf, row(ln_kv), bf(w_kv), pair(g_k), d // LANES, 1.0,
                                   segment_major=True)
    return xf.reshape(batch, seq, d)
```

```python
import functools

import jax
import jax.numpy as jnp
from jax import lax
from jax.experimental import pallas as pl
from jax.experimental.pallas import tpu as pltpu

EPS = 1e-6
CHUNK = 128
HEAD_DIM = 64
LANES = 128
TOKEN_TILE = 512
ATTN_BLOCK = 128
STATIC_SWEEP = 3
FF_CHUNK = 1024
VMEM_LIMIT = 48 * 1024 * 1024
SUBLANES = 8
KEY_SEGMENT = ATTN_BLOCK // SUBLANES
LOG2_E = 1.4426950408889634
WEIGHT_FLOOR = 2.0 ** -126

_BF16 = jnp.bfloat16
_F32 = jnp.float32


def _rms(x, g):
    return x * lax.rsqrt(jnp.mean(x * x, axis=-1, keepdims=True) + EPS) * g


def _gelu_tanh(x):
    c = -2.0 * LOG2_E * (2.0 / jnp.pi) ** 0.5
    return x / (1.0 + jnp.exp2(x * (x * x * (c * 0.044715) + c)))


def _layer_spec(stacked, layer):
    rest = stacked.shape[1:]
    return pl.BlockSpec((None,) + rest, lambda *_: (layer,) + (0,) * len(rest),
                        pipeline_mode=pl.Buffered(1))


def _params(n_axes):
    return pltpu.CompilerParams(
        dimension_semantics=("arbitrary",) * n_axes, vmem_limit_bytes=VMEM_LIMIT)


def _sgu_kernel(x_ref, ln_ref, win_ref, gv_ref, ws_ref, bias_ref, y_ref):
    tm, d = x_ref.shape
    n_groups = ws_ref.shape[0]
    n_chunks = tm // CHUNK
    h = _rms(x_ref[...], ln_ref[...]).astype(_BF16)
    z = _gelu_tanh(jnp.dot(h, win_ref[...], preferred_element_type=_F32))
    u = z[:, :d]
    v = _rms(z[:, d:], gv_ref[...]).astype(_BF16)
    row = lax.broadcasted_iota(jnp.int32, (CHUNK, CHUNK), 0)
    col = lax.broadcasted_iota(jnp.int32, (CHUNK, CHUNK), 1)
    causal = col <= row
    mix_cols = []
    for g in range(n_groups):
        w = jnp.where(causal, ws_ref[g], 0.0).astype(_BF16)
        vg = jnp.concatenate(
            [v[c * CHUNK:(c + 1) * CHUNK, g * LANES:(g + 1) * LANES] for c in range(n_chunks)],
            axis=1)
        mg = jnp.dot(w, vg, preferred_element_type=_F32)
        mix_cols.append(jnp.concatenate(
            [mg[:, c * LANES:(c + 1) * LANES] for c in range(n_chunks)], axis=0))
    mix = jnp.concatenate(mix_cols, axis=1)
    bias = jnp.concatenate([bias_ref[...]] * n_chunks, axis=0)
    y_ref[...] = (u * (mix + bias)).astype(y_ref.dtype)


def _sgu_call(x, layer, ln, w_in, g_v, w_s, bias):
    m, d = x.shape
    tm = TOKEN_TILE
    return pl.pallas_call(
        _sgu_kernel,
        grid=(m // tm,),
        in_specs=[pl.BlockSpec((tm, d), lambda i: (i, 0))]
        + [_layer_spec(a, layer) for a in (ln, w_in, g_v, w_s, bias)],
        out_specs=pl.BlockSpec((tm, d), lambda i: (i, 0)),
        out_shape=jax.ShapeDtypeStruct((m, d), _BF16),
        compiler_params=_params(1),
        name="sgu_mixer",
    )(x, ln, w_in, g_v, w_s, bias)


def _post_kernel(x_ref, y_ref, p_ref, wo_ref, lnm_ref, wup_ref, wdn_ref,
                 lnp_ref, wg_ref, wp_ref, o_ref):
    d_ff = wup_ref.shape[1]
    x = x_ref[...] + jnp.dot(y_ref[...], wo_ref[...], preferred_element_type=_F32)
    h = _rms(x, lnm_ref[...]).astype(_BF16)
    acc = x
    for c in range(d_ff // FF_CHUNK):
        a = jnp.dot(h, wup_ref[:, c * FF_CHUNK:(c + 1) * FF_CHUNK], preferred_element_type=_F32)
        a = jnp.maximum(a, 0.0)
        a = (a * a).astype(_BF16)
        acc = acc + jnp.dot(a, wdn_ref[c * FF_CHUNK:(c + 1) * FF_CHUNK, :],
                            preferred_element_type=_F32)
    x = acc
    hg = _rms(x, lnp_ref[...]).astype(_BF16)
    gate = 1.0 / (1.0 + jnp.exp(-jnp.dot(hg, wg_ref[...], preferred_element_type=_F32)))
    pp = jnp.dot(p_ref[...].astype(_BF16), wp_ref[...], preferred_element_type=_F32)
    o_ref[...] = x + pp * gate


def _post_call(x, y, layer, p, mixer_layer, w_out, ln_mlp, w_up, w_down, ln_ple, w_gate, w_proj):
    m, d = x.shape
    tm = TOKEN_TILE
    return pl.pallas_call(
        _post_kernel,
        grid=(m // tm,),
        in_specs=[
            pl.BlockSpec((tm, d), lambda i: (i, 0)),
            pl.BlockSpec((tm, d), lambda i: (i, 0)),
            pl.BlockSpec((None, tm, p.shape[2]), lambda i: (layer, i, 0)),
            _layer_spec(w_out, mixer_layer),
        ] + [_layer_spec(a, layer) for a in (ln_mlp, w_up, w_down, ln_ple, w_gate, w_proj)],
        out_specs=pl.BlockSpec((tm, d), lambda i: (i, 0)),
        out_shape=jax.ShapeDtypeStruct((m, d), _F32),
        compiler_params=_params(1),
        name="post_mlp_ple",
    )(x, y, p, w_out, ln_mlp, w_up, w_down, ln_ple, w_gate, w_proj)


def _proj_kernel(x_ref, ln_ref, w_ref, g_ref, o_ref, *, n_norm, out_scale, segment_major):
    tm = x_ref.shape[0]
    h = _rms(x_ref[...], ln_ref[...]).astype(_BF16)
    if segment_major:
        r = lax.broadcasted_iota(jnp.int32, (ATTN_BLOCK, ATTN_BLOCK), 0)
        c = lax.broadcasted_iota(jnp.int32, (ATTN_BLOCK, ATTN_BLOCK), 1)
        src = KEY_SEGMENT * (r & (SUBLANES - 1)) + r // SUBLANES
        perm = jnp.where(c == src, 1.0, 0.0).astype(_BF16)
        h = jnp.concatenate(
            [jnp.dot(perm, h[b * ATTN_BLOCK:(b + 1) * ATTN_BLOCK],
                     preferred_element_type=_F32).astype(_BF16)
             for b in range(tm // ATTN_BLOCK)], axis=0)
    y = jnp.dot(h, w_ref[...], preferred_element_type=_F32)
    first_head = lax.broadcasted_iota(jnp.int32, (tm, LANES), 1) < HEAD_DIM
    for t in range(y.shape[1] // LANES):
        blk = y[:, t * LANES:(t + 1) * LANES]
        if t < n_norm:
            sq = blk * blk
            s_a = jnp.sum(jnp.where(first_head, sq, 0.0), axis=-1, keepdims=True)
            s_b = jnp.sum(jnp.where(first_head, 0.0, sq), axis=-1, keepdims=True)
            ms = jnp.where(first_head, s_a, s_b) * (1.0 / HEAD_DIM)
            blk = blk * lax.rsqrt(ms + EPS) * g_ref[...] * out_scale
        o_ref[:, t * LANES:(t + 1) * LANES] = blk.astype(o_ref.dtype)


def _proj_call(x, layer, ln, w, g_pair, n_norm, out_scale, segment_major=False):
    m, d = x.shape
    n = w.shape[2]
    tm = TOKEN_TILE
    return pl.pallas_call(
        functools.partial(_proj_kernel, n_norm=n_norm, out_scale=out_scale,
                          segment_major=segment_major),
        grid=(m // tm,),
        in_specs=[pl.BlockSpec((tm, d), lambda i: (i, 0))]
        + [_layer_spec(a, layer) for a in (ln, w, g_pair)],
        out_specs=pl.BlockSpec((tm, n), lambda i: (i, 0)),
        out_shape=jax.ShapeDtypeStruct((m, n), _BF16),
        compiler_params=_params(1),
        name="head_proj",
    )(x, ln, w, g_pair)


def _attn_kernel(q_ref, k_ref, v_ref, o_ref, qs_ref, passed_ref, acc_ref):
    blk = ATTN_BLOCK
    n_pairs = qs_ref.shape[0]
    n_groups = blk // SUBLANES
    lane = lax.broadcasted_iota(jnp.int32, (blk, LANES), 1)
    first_head = lane < HEAD_DIM
    sub = lax.broadcasted_iota(jnp.int32, (SUBLANES, 2 * blk), 0)

    def block_weights(z, passed, valid):
        beta = 1.0 / (1.0 + jnp.exp2(z))
        keep = 1.0 - beta
        if valid is not None:
            keep = jnp.where(valid, keep, 1.0)
        after = [None] * n_groups
        run = None
        for v in range(n_groups - 1, -1, -1):
            after[v] = run
            kv = keep[v * SUBLANES:(v + 1) * SUBLANES]
            run = kv if run is None else run * kv
        later = jnp.where(sub + 1 < SUBLANES, pltpu.roll(run, SUBLANES - 1, 0), 1.0)
        for step in (1, 2, 4):
            shifted = pltpu.roll(later, SUBLANES - step, 0)
            later = later * jnp.where(sub + step < SUBLANES, shifted, 1.0)
        whole = later * run
        base = later if passed is None else later * passed
        parts = []
        for v in range(n_groups):
            scale = base if after[v] is None else after[v] * base
            parts.append(beta[v * SUBLANES:(v + 1) * SUBLANES] * scale)
        w = jnp.concatenate(parts, axis=0)
        if valid is not None:
            w = jnp.where(valid, w, 0.0)
        total = jnp.broadcast_to(whole[0:1], whole.shape)
        return w.astype(_BF16), total if passed is None else passed * total

    def sweep(j_last, n_blocks, first):
        rows = pl.ds(pl.multiple_of((j_last - (n_blocks - 1)) * blk, blk), n_blocks * blk)
        valid = None
        if first:
            r = lax.broadcasted_iota(jnp.int32, (blk, 2 * blk), 0)
            key = KEY_SEGMENT * (r & (SUBLANES - 1)) + r // SUBLANES
            query = lax.broadcasted_iota(jnp.int32, (blk, 2 * blk), 1) & (blk - 1)
            valid = key < query
        pairs = range(n_pairs)
        zs = [lax.dot_general(k_ref[rows, p * LANES:(p + 1) * LANES], qs_ref[p],
                              (((1,), (1,)), ((), ())), preferred_element_type=_F32)
              for p in pairs]
        weights = []
        for p in pairs:
            passed = None if first else passed_ref[p]
            parts = [None] * n_blocks
            for b in range(n_blocks - 1, -1, -1):
                parts[b], passed = block_weights(
                    zs[p][b * blk:(b + 1) * blk], passed,
                    valid if first and b == n_blocks - 1 else None)
            passed_ref[p] = passed
            weights.append(jnp.concatenate(parts, axis=0) if n_blocks > 1 else parts[0])
        for p in pairs:
            acc = lax.dot_general(v_ref[rows, p * LANES:(p + 1) * LANES], weights[p],
                                  (((0,), (0,)), ((), ())), preferred_element_type=_F32)
            acc_ref[p] = acc if first else acc_ref[p] + acc

    def largest_passed():
        top = passed_ref[0]
        for p in range(1, n_pairs):
            top = jnp.maximum(top, passed_ref[p])
        return jnp.max(top)

    def query_block(i, carry):
        rows = pl.ds(pl.multiple_of(i * blk, blk), blk)
        for p in range(n_pairs):
            q2 = q_ref[rows, p * LANES:(p + 1) * LANES]
            zeros = jnp.zeros_like(q2)
            qs_ref[p] = jnp.concatenate(
                [jnp.where(first_head, q2, zeros), jnp.where(first_head, zeros, q2)], axis=0)

        for n in range(1, STATIC_SWEEP + 1):
            @pl.when((i == n - 1) if n < STATIC_SWEEP else (i >= n - 1))
            def _(n=n):
                sweep(i, n, first=True)

        def cond(state):
            j, top = state
            return jnp.logical_and(j >= 0, top >= WEIGHT_FLOOR)

        def body(state):
            j, _ = state
            sweep(j, 1, first=False)
            return j - 1, largest_passed()

        lax.while_loop(cond, body, (i - STATIC_SWEEP, largest_passed()))
        for p in range(n_pairs):
            acc = acc_ref[p]
            out_t = jnp.concatenate([acc[:HEAD_DIM, :blk], acc[HEAD_DIM:, blk:]], axis=0)
            o_ref[rows, p * LANES:(p + 1) * LANES] = out_t.T.astype(o_ref.dtype)
        return carry

    lax.fori_loop(0, q_ref.shape[0] // blk, query_block, 0)


def _attn_call(q, kv, batch, seq):
    m, d = q.shape
    blk = ATTN_BLOCK
    n_pairs = d // LANES
    return pl.pallas_call(
        _attn_kernel,
        grid=(batch,),
        in_specs=[
            pl.BlockSpec((seq, d), lambda b: (b, 0)),
            pl.BlockSpec((seq, d), lambda b: (b, 0)),
            pl.BlockSpec((seq, d), lambda b: (b, 1)),
        ],
        out_specs=pl.BlockSpec((seq, d), lambda b: (b, 0)),
        out_shape=jax.ShapeDtypeStruct((m, d), _BF16),
        scratch_shapes=[
            pltpu.VMEM((n_pairs, 2 * blk, LANES), _BF16),
            pltpu.VMEM((n_pairs, SUBLANES, 2 * blk), _F32),
            pltpu.VMEM((n_pairs, LANES, 2 * blk), _F32),
        ],
        compiler_params=_params(1),
        name="stick_breaking",
    )(q, kv, kv)


def kernel(x, p, ln_mix_a, w_in_a, g_v_a, w_spatial, b_spatial, w_out_a, ln_kv, w_kv, g_k,
           ln_mix_b, w_q, g_q, w_out_b, ln_mlp, w_up, w_down, ln_ple, w_ple_gate, w_ple_proj):
    batch, seq, d = x.shape
    depth = p.shape[0]
    n_a = ln_mix_a.shape[0]
    m = batch * seq
    xf = x.reshape(m, d)
    pf = p.reshape(depth, m, p.shape[-1])
    bf = lambda w: w.astype(_BF16)
    rows = lambda g: g.reshape(g.shape[0], 1, g.shape[1])
    pairs = lambda g: jnp.concatenate([g, g], axis=-1).reshape(g.shape[0], 1, LANES)
    ln_mix_a, g_v_a, ln_mix_b, ln_mlp, ln_ple = map(rows, (ln_mix_a, g_v_a, ln_mix_b, ln_mlp, ln_ple))
    w_in_a, w_out_a, w_q, w_out_b, w_up, w_down, w_ple_gate, w_ple_proj = map(
        bf, (w_in_a, w_out_a, w_q, w_out_b, w_up, w_down, w_ple_gate, w_ple_proj))
    bias = jnp.repeat(jnp.swapaxes(b_spatial, 1, 2), d // b_spatial.shape[1], axis=2)
    kv_shared = None
    for i in range(depth):
        if i < n_a:
            y = _sgu_call(xf, i, ln_mix_a, w_in_a, g_v_a, w_spatial, bias)
            mixer_layer, w_o = i, w_out_a
        else:
            j = i - n_a
            q = _proj_call(xf, j, ln_mix_b, w_q, pairs(g_q), d // LANES,
                           -LOG2_E * HEAD_DIM ** -0.5)
            y = _attn_call(q, kv_shared, batch, seq)
            mixer_layer, w_o = j, w_out_b
        xf = _post_call(xf, y, i, pf, mixer_layer, w_o, ln_mlp, w_up, w_down, ln_ple,
                        w_ple_gate, w_ple_proj)
        if i == n_a - 1:
            kv_shared = _proj_call(xf, 0, rows(ln_kv[None]), bf(w_kv)[None], pairs(g_k[None]),
                                   d // LANES, 1.0, segment_major=True)
    return xf.reshape(batch, seq, d)
```

```python
import functools

import jax
import jax.numpy as jnp
from jax import lax
from jax.experimental import pallas as pl
from jax.experimental.pallas import tpu as pltpu

EPS = 1e-6
CHUNK = 128
HEAD_DIM = 64
LANES = 128
MXU_WIDTH = 256
TOKEN_TILE = 512
WIDE_TOKEN_TILE = 1024
ATTN_BLOCK = 128
STATIC_SWEEP = 3
FF_CHUNK = 1024
VMEM_LIMIT = 48 * 1024 * 1024
SUBLANES = 8
KEY_SEGMENT = ATTN_BLOCK // SUBLANES
LOG2_E = 1.4426950408889634
WEIGHT_FLOOR = 2.0 ** -126

_BF16 = jnp.bfloat16
_F32 = jnp.float32


def _rms(x, g):
    return x * lax.rsqrt(jnp.mean(x * x, axis=-1, keepdims=True) + EPS) * g


def _gelu_tanh(x):
    c = -2.0 * LOG2_E * (2.0 / jnp.pi) ** 0.5
    return x / (1.0 + jnp.exp2(x * (x * x * (c * 0.044715) + c)))


def _layer_spec(stacked, layer):
    rest = stacked.shape[1:]
    return pl.BlockSpec((None,) + rest, lambda *_: (layer,) + (0,) * len(rest),
                        pipeline_mode=pl.Buffered(1))


def _params(n_axes):
    return pltpu.CompilerParams(
        dimension_semantics=("arbitrary",) * n_axes, vmem_limit_bytes=VMEM_LIMIT)


def _sgu_kernel(x_ref, ln_ref, win_ref, gv_ref, ws_ref, bias_ref, y_ref):
    tm, d = x_ref.shape
    n_groups = ws_ref.shape[0]
    n_chunks = tm // CHUNK
    h = _rms(x_ref[...], ln_ref[...]).astype(_BF16)
    v = _gelu_tanh(jnp.dot(h, win_ref[:, d:], preferred_element_type=_F32))
    v = _rms(v, gv_ref[...]).astype(_BF16)
    u = _gelu_tanh(jnp.dot(h, win_ref[:, :d], preferred_element_type=_F32))
    row = lax.broadcasted_iota(jnp.int32, (CHUNK, CHUNK), 0)
    col = lax.broadcasted_iota(jnp.int32, (CHUNK, CHUNK), 1)
    causal = col <= row
    mix_cols = []
    for g in range(n_groups):
        w = jnp.where(causal, ws_ref[g], 0.0).astype(_BF16)
        vg = jnp.concatenate(
            [v[c * CHUNK:(c + 1) * CHUNK, g * LANES:(g + 1) * LANES] for c in range(n_chunks)],
            axis=1)
        mg = jnp.dot(w, vg, preferred_element_type=_F32)
        mix_cols.append(jnp.concatenate(
            [mg[:, c * LANES:(c + 1) * LANES] for c in range(n_chunks)], axis=0))
    mix = jnp.concatenate(mix_cols, axis=1)
    bias = jnp.concatenate([bias_ref[...]] * n_chunks, axis=0)
    y_ref[...] = (u * (mix + bias)).astype(y_ref.dtype)


def _sgu_call(x, layer, ln, w_in, g_v, w_s, bias):
    m, d = x.shape
    tm = WIDE_TOKEN_TILE
    return pl.pallas_call(
        _sgu_kernel,
        grid=(m // tm,),
        in_specs=[pl.BlockSpec((tm, d), lambda i: (i, 0))]
        + [_layer_spec(a, layer) for a in (ln, w_in, g_v, w_s, bias)],
        out_specs=pl.BlockSpec((tm, d), lambda i: (i, 0)),
        out_shape=jax.ShapeDtypeStruct((m, d), _BF16),
        compiler_params=_params(1),
        name="sgu_mixer",
    )(x, ln, w_in, g_v, w_s, bias)


def _post_kernel(x_ref, y_ref, p_ref, wo_ref, lnm_ref, wup_ref, wdn_ref,
                 lnp_ref, wg_ref, wp_ref, o_ref):
    d_ff = wup_ref.shape[1]
    x = x_ref[...] + jnp.dot(y_ref[...], wo_ref[...], preferred_element_type=_F32)
    h = _rms(x, lnm_ref[...]).astype(_BF16)
    acc = x
    for c in range(d_ff // FF_CHUNK):
        a = jnp.dot(h, wup_ref[:, c * FF_CHUNK:(c + 1) * FF_CHUNK], preferred_element_type=_F32)
        a = jnp.maximum(a, 0.0)
        a = (a * a).astype(_BF16)
        acc = acc + jnp.dot(a, wdn_ref[c * FF_CHUNK:(c + 1) * FF_CHUNK, :],
                            preferred_element_type=_F32)
    x = acc
    hg = _rms(x, lnp_ref[...]).astype(_BF16)
    gate = 1.0 / (1.0 + jnp.exp(-jnp.dot(hg, wg_ref[...], preferred_element_type=_F32)))
    pp = jnp.dot(p_ref[...].astype(_BF16), wp_ref[...], preferred_element_type=_F32)
    o_ref[...] = x + pp * gate


def _post_call(x, y, layer, p, mixer_layer, w_out, ln_mlp, w_up, w_down, ln_ple, w_gate, w_proj):
    m, d = x.shape
    tm = TOKEN_TILE
    return pl.pallas_call(
        _post_kernel,
        grid=(m // tm,),
        in_specs=[
            pl.BlockSpec((tm, d), lambda i: (i, 0)),
            pl.BlockSpec((tm, d), lambda i: (i, 0)),
            pl.BlockSpec((None, tm, p.shape[2]), lambda i: (layer, i, 0)),
            _layer_spec(w_out, mixer_layer),
        ] + [_layer_spec(a, layer) for a in (ln_mlp, w_up, w_down, ln_ple, w_gate, w_proj)],
        out_specs=pl.BlockSpec((tm, d), lambda i: (i, 0)),
        out_shape=jax.ShapeDtypeStruct((m, d), _F32),
        compiler_params=_params(1),
        name="post_mlp_ple",
    )(x, y, p, w_out, ln_mlp, w_up, w_down, ln_ple, w_gate, w_proj)


def _proj_kernel(x_ref, ln_ref, w_ref, g_ref, o_ref, *, n_norm, out_scale, segment_major):
    tm = x_ref.shape[0]
    h = _rms(x_ref[...], ln_ref[...]).astype(_BF16)
    if segment_major:
        r = lax.broadcasted_iota(jnp.int32, (ATTN_BLOCK, ATTN_BLOCK), 0)
        c = lax.broadcasted_iota(jnp.int32, (ATTN_BLOCK, ATTN_BLOCK), 1)
        src = KEY_SEGMENT * (r & (SUBLANES - 1)) + r // SUBLANES
        perm = jnp.where(c == src, 1.0, 0.0).astype(_BF16)
        h = jnp.concatenate(
            [jnp.dot(perm, h[b * ATTN_BLOCK:(b + 1) * ATTN_BLOCK],
                     preferred_element_type=_F32).astype(_BF16)
             for b in range(tm // ATTN_BLOCK)], axis=0)
    first_head = lax.broadcasted_iota(jnp.int32, (tm, LANES), 1) < HEAD_DIM
    gain = g_ref[...] * out_scale
    for c in range(w_ref.shape[1] // MXU_WIDTH):
        y = jnp.dot(h, w_ref[:, c * MXU_WIDTH:(c + 1) * MXU_WIDTH], preferred_element_type=_F32)
        for t in range(MXU_WIDTH // LANES):
            blk = y[:, t * LANES:(t + 1) * LANES]
            col = c * MXU_WIDTH + t * LANES
            if col < n_norm * LANES:
                sq = blk * blk
                s_a = jnp.sum(jnp.where(first_head, sq, 0.0), axis=-1, keepdims=True)
                s_b = jnp.sum(jnp.where(first_head, 0.0, sq), axis=-1, keepdims=True)
                ms = jnp.where(first_head, s_a, s_b) * (1.0 / HEAD_DIM)
                blk = blk * (lax.rsqrt(ms + EPS) * gain)
            o_ref[:, col:col + LANES] = blk.astype(o_ref.dtype)


def _proj_call(x, layer, ln, w, g_pair, n_norm, out_scale, segment_major=False):
    m, d = x.shape
    n = w.shape[2]
    tm = WIDE_TOKEN_TILE
    return pl.pallas_call(
        functools.partial(_proj_kernel, n_norm=n_norm, out_scale=out_scale,
                          segment_major=segment_major),
        grid=(m // tm,),
        in_specs=[pl.BlockSpec((tm, d), lambda i: (i, 0))]
        + [_layer_spec(a, layer) for a in (ln, w, g_pair)],
        out_specs=pl.BlockSpec((tm, n), lambda i: (i, 0)),
        out_shape=jax.ShapeDtypeStruct((m, n), _BF16),
        compiler_params=_params(1),
        name="head_proj",
    )(x, ln, w, g_pair)


def _attn_kernel(q_ref, k_ref, v_ref, o_ref, qs_ref, passed_ref, acc_ref):
    blk = ATTN_BLOCK
    n_pairs = qs_ref.shape[0]
    n_groups = blk // SUBLANES
    lane = lax.broadcasted_iota(jnp.int32, (blk, LANES), 1)
    first_head = lane < HEAD_DIM
    sub = lax.broadcasted_iota(jnp.int32, (SUBLANES, 2 * blk), 0)

    def block_weights(z, passed, valid):
        beta = 1.0 / (1.0 + jnp.exp2(z))
        keep = 1.0 - beta
        if valid is not None:
            keep = jnp.where(valid, keep, 1.0)
        after = [None] * n_groups
        run = None
        for v in range(n_groups - 1, -1, -1):
            after[v] = run
            kv = keep[v * SUBLANES:(v + 1) * SUBLANES]
            run = kv if run is None else run * kv
        later = jnp.where(sub + 1 < SUBLANES, pltpu.roll(run, SUBLANES - 1, 0), 1.0)
        for step in (1, 2, 4):
            shifted = pltpu.roll(later, SUBLANES - step, 0)
            later = later * jnp.where(sub + step < SUBLANES, shifted, 1.0)
        whole = later * run
        base = later if passed is None else later * passed
        parts = []
        for v in range(n_groups):
            scale = base if after[v] is None else after[v] * base
            parts.append(beta[v * SUBLANES:(v + 1) * SUBLANES] * scale)
        w = jnp.concatenate(parts, axis=0)
        if valid is not None:
            w = jnp.where(valid, w, 0.0)
        total = jnp.broadcast_to(whole[0:1], whole.shape)
        return w.astype(_BF16), total if passed is None else passed * total

    def sweep(j_last, n_blocks, first):
        rows = pl.ds(pl.multiple_of((j_last - (n_blocks - 1)) * blk, blk), n_blocks * blk)
        valid = None
        if first:
            r = lax.broadcasted_iota(jnp.int32, (blk, 2 * blk), 0)
            key = KEY_SEGMENT * (r & (SUBLANES - 1)) + r // SUBLANES
            query = lax.broadcasted_iota(jnp.int32, (blk, 2 * blk), 1) & (blk - 1)
            valid = key < query
        pairs = range(n_pairs)
        zs = [lax.dot_general(k_ref[rows, p * LANES:(p + 1) * LANES], qs_ref[p],
                              (((1,), (1,)), ((), ())), preferred_element_type=_F32)
              for p in pairs]
        weights = []
        for p in pairs:
            passed = None if first else passed_ref[p]
            parts = [None] * n_blocks
            for b in range(n_blocks - 1, -1, -1):
                parts[b], passed = block_weights(
                    zs[p][b * blk:(b + 1) * blk], passed,
                    valid if first and b == n_blocks - 1 else None)
            passed_ref[p] = passed
            weights.append(jnp.concatenate(parts, axis=0) if n_blocks > 1 else parts[0])
        for p in pairs:
            acc = lax.dot_general(v_ref[rows, p * LANES:(p + 1) * LANES], weights[p],
                                  (((0,), (0,)), ((), ())), preferred_element_type=_F32)
            acc_ref[p] = acc if first else acc_ref[p] + acc

    def largest_passed():
        top = passed_ref[0]
        for p in range(1, n_pairs):
            top = jnp.maximum(top, passed_ref[p])
        return jnp.max(top)

    def query_block(i, carry):
        rows = pl.ds(pl.multiple_of(i * blk, blk), blk)
        for p in range(n_pairs):
            q2 = q_ref[rows, p * LANES:(p + 1) * LANES]
            zeros = jnp.zeros_like(q2)
            qs_ref[p] = jnp.concatenate(
                [jnp.where(first_head, q2, zeros), jnp.where(first_head, zeros, q2)], axis=0)

        for n in range(1, STATIC_SWEEP + 1):
            @pl.when((i == n - 1) if n < STATIC_SWEEP else (i >= n - 1))
            def _(n=n):
                sweep(i, n, first=True)

        def cond(state):
            j, top = state
            return jnp.logical_and(j >= 0, top >= WEIGHT_FLOOR)

        def body(state):
            j, _ = state
            sweep(j, 1, first=False)
            return j - 1, largest_passed()

        lax.while_loop(cond, body, (i - STATIC_SWEEP, largest_passed()))
        for p in range(n_pairs):
            acc = acc_ref[p]
            out_t = jnp.concatenate([acc[:HEAD_DIM, :blk], acc[HEAD_DIM:, blk:]], axis=0)
            o_ref[rows, p * LANES:(p + 1) * LANES] = out_t.T.astype(o_ref.dtype)
        return carry

    lax.fori_loop(0, q_ref.shape[0] // blk, query_block, 0)


def _attn_call(q, kv, batch, seq):
    m, d = q.shape
    blk = ATTN_BLOCK
    n_pairs = d // LANES
    return pl.pallas_call(
        _attn_kernel,
        grid=(batch,),
        in_specs=[
            pl.BlockSpec((seq, d), lambda b: (b, 0)),
            pl.BlockSpec((seq, d), lambda b: (b, 0)),
            pl.BlockSpec((seq, d), lambda b: (b, 1)),
        ],
        out_specs=pl.BlockSpec((seq, d), lambda b: (b, 0)),
        out_shape=jax.ShapeDtypeStruct((m, d), _BF16),
        scratch_shapes=[
            pltpu.VMEM((n_pairs, 2 * blk, LANES), _BF16),
            pltpu.VMEM((n_pairs, SUBLANES, 2 * blk), _F32),
            pltpu.VMEM((n_pairs, LANES, 2 * blk), _F32),
        ],
        compiler_params=_params(1),
        name="stick_breaking",
    )(q, kv, kv)


def kernel(x, p, ln_mix_a, w_in_a, g_v_a, w_spatial, b_spatial, w_out_a, ln_kv, w_kv, g_k,
           ln_mix_b, w_q, g_q, w_out_b, ln_mlp, w_up, w_down, ln_ple, w_ple_gate, w_ple_proj):
    batch, seq, d = x.shape
    depth = p.shape[0]
    n_a = ln_mix_a.shape[0]
    m = batch * seq
    xf = x.reshape(m, d)
    pf = p.reshape(depth, m, p.shape[-1])
    bf = lambda w: w.astype(_BF16)
    rows = lambda g: g.reshape(g.shape[0], 1, g.shape[1])
    pairs = lambda g: jnp.concatenate([g, g], axis=-1).reshape(g.shape[0], 1, LANES)
    ln_mix_a, g_v_a, ln_mix_b, ln_mlp, ln_ple = map(rows, (ln_mix_a, g_v_a, ln_mix_b, ln_mlp, ln_ple))
    w_in_a, w_out_a, w_q, w_out_b, w_up, w_down, w_ple_gate, w_ple_proj = map(
        bf, (w_in_a, w_out_a, w_q, w_out_b, w_up, w_down, w_ple_gate, w_ple_proj))
    bias = jnp.repeat(jnp.swapaxes(b_spatial, 1, 2), d // b_spatial.shape[1], axis=2)
    kv_shared = None
    for i in range(depth):
        if i < n_a:
            y = _sgu_call(xf, i, ln_mix_a, w_in_a, g_v_a, w_spatial, bias)
            mixer_layer, w_o = i, w_out_a
        else:
            j = i - n_a
            q = _proj_call(xf, j, ln_mix_b, w_q, pairs(g_q), d // LANES,
                           -LOG2_E * HEAD_DIM ** -0.5)
            y = _attn_call(q, kv_shared, batch, seq)
            mixer_layer, w_o = j, w_out_b
        xf = _post_call(xf, y, i, pf, mixer_layer, w_o, ln_mlp, w_up, w_down, ln_ple,
                        w_ple_gate, w_ple_proj)
        if i == n_a - 1:
            kv_shared = _proj_call(xf, 0, rows(ln_kv[None]), bf(w_kv)[None], pairs(g_k[None]),
                                   d // LANES, 1.0, segment_major=True)
    return xf.reshape(batch, seq, d)
```

```python
import functools

import jax
import jax.numpy as jnp
from jax import lax
from jax.experimental import pallas as pl
from jax.experimental.pallas import tpu as pltpu

EPS = 1e-6
CHUNK = 128
HEAD_DIM = 64
LANES = 128
MXU_WIDTH = 256
TOKEN_TILE = 512
WIDE_TOKEN_TILE = 1024
ATTN_BLOCK = 128
STATIC_SWEEP = 3
FF_CHUNK = 1024
VMEM_LIMIT = 56 * 1024 * 1024
SUBLANES = 8
KEY_SEGMENT = ATTN_BLOCK // SUBLANES
LOG2_E = 1.4426950408889634
WEIGHT_FLOOR = 2.0 ** -126

_BF16 = jnp.bfloat16
_F32 = jnp.float32


def _rms(x, g):
    return x * lax.rsqrt(jnp.mean(x * x, axis=-1, keepdims=True) + EPS) * g


def _gelu_tanh(x):
    c = -2.0 * LOG2_E * (2.0 / jnp.pi) ** 0.5
    return x / (1.0 + jnp.exp2(x * (x * x * (c * 0.044715) + c)))


def _layer_spec(stacked, layer):
    rest = stacked.shape[1:]
    return pl.BlockSpec((None,) + rest, lambda *_: (layer,) + (0,) * len(rest),
                        pipeline_mode=pl.Buffered(1))


def _params(n_axes):
    return pltpu.CompilerParams(
        dimension_semantics=("arbitrary",) * n_axes, vmem_limit_bytes=VMEM_LIMIT)


def _sgu_kernel(x_ref, ln_ref, win_ref, gv_ref, ws_ref, bias_ref, y_ref):
    tm, d = x_ref.shape
    n_groups = ws_ref.shape[0]
    n_chunks = tm // CHUNK
    h = _rms(x_ref[...], ln_ref[...]).astype(_BF16)
    v = _gelu_tanh(jnp.dot(h, win_ref[:, d:], preferred_element_type=_F32))
    v = _rms(v, gv_ref[...]).astype(_BF16)
    u = _gelu_tanh(jnp.dot(h, win_ref[:, :d], preferred_element_type=_F32))
    row = lax.broadcasted_iota(jnp.int32, (CHUNK, CHUNK), 0)
    col = lax.broadcasted_iota(jnp.int32, (CHUNK, CHUNK), 1)
    causal = col <= row
    mix_cols = []
    for g in range(n_groups):
        w = jnp.where(causal, ws_ref[g], 0.0).astype(_BF16)
        vg = jnp.concatenate(
            [v[c * CHUNK:(c + 1) * CHUNK, g * LANES:(g + 1) * LANES] for c in range(n_chunks)],
            axis=1)
        mg = jnp.dot(w, vg, preferred_element_type=_F32)
        mix_cols.append(jnp.concatenate(
            [mg[:, c * LANES:(c + 1) * LANES] for c in range(n_chunks)], axis=0))
    mix = jnp.concatenate(mix_cols, axis=1)
    bias = jnp.concatenate([bias_ref[...]] * n_chunks, axis=0)
    y_ref[...] = (u * (mix + bias)).astype(y_ref.dtype)


def _sgu_call(x, layer, ln, w_in, g_v, w_s, bias):
    m, d = x.shape
    tm = WIDE_TOKEN_TILE
    return pl.pallas_call(
        _sgu_kernel,
        grid=(m // tm,),
        in_specs=[pl.BlockSpec((tm, d), lambda i: (i, 0))]
        + [_layer_spec(a, layer) for a in (ln, w_in, g_v, w_s, bias)],
        out_specs=pl.BlockSpec((tm, d), lambda i: (i, 0)),
        out_shape=jax.ShapeDtypeStruct((m, d), _BF16),
        compiler_params=_params(1),
        name="sgu_mixer",
    )(x, ln, w_in, g_v, w_s, bias)


def _post_kernel(x_ref, y_ref, p_ref, wo_ref, lnm_ref, wup_ref, wdn_ref,
                 lnp_ref, wg_ref, wp_ref, *rest, head_projs):
    n_proj = len(head_projs)
    o_ref = rest[3 * n_proj]
    d_ff = wup_ref.shape[1]
    x = x_ref[...] + jnp.dot(y_ref[...], wo_ref[...], preferred_element_type=_F32)
    h = _rms(x, lnm_ref[...]).astype(_BF16)
    acc = x
    for c in range(d_ff // FF_CHUNK):
        a = jnp.dot(h, wup_ref[:, c * FF_CHUNK:(c + 1) * FF_CHUNK], preferred_element_type=_F32)
        a = jnp.maximum(a, 0.0)
        a = (a * a).astype(_BF16)
        acc = acc + jnp.dot(a, wdn_ref[c * FF_CHUNK:(c + 1) * FF_CHUNK, :],
                            preferred_element_type=_F32)
    x = acc
    hg = _rms(x, lnp_ref[...]).astype(_BF16)
    gate = 1.0 / (1.0 + jnp.exp(-jnp.dot(hg, wg_ref[...], preferred_element_type=_F32)))
    pp = jnp.dot(p_ref[...].astype(_BF16), wp_ref[...], preferred_element_type=_F32)
    x = x + pp * gate
    o_ref[...] = x
    for n, cfg in enumerate(head_projs):
        _project_heads(x, *rest[3 * n:3 * n + 3], rest[3 * n_proj + 1 + n], **cfg)


def _post_call(x, y, layer, p, mixer_layer, w_out, ln_mlp, w_up, w_down, ln_ple, w_gate, w_proj,
               head_projs=()):
    m, d = x.shape
    tm = TOKEN_TILE
    tile = lambda width: pl.BlockSpec((tm, width), lambda i: (i, 0))
    proj_arrays, proj_specs, cfgs = [], [], []
    for hp in head_projs:
        arrays = (hp["ln"], hp["w"], hp["gain"])
        proj_arrays += arrays
        proj_specs += [_layer_spec(a, hp["layer"]) for a in arrays]
        cfgs.append({k: hp[k] for k in ("n_norm", "out_scale", "segment_major")})
    widths = [hp["w"].shape[2] for hp in head_projs]
    return pl.pallas_call(
        functools.partial(_post_kernel, head_projs=tuple(cfgs)),
        grid=(m // tm,),
        in_specs=[
            tile(d), tile(d),
            pl.BlockSpec((None, tm, p.shape[2]), lambda i: (layer, i, 0)),
            _layer_spec(w_out, mixer_layer),
        ] + [_layer_spec(a, layer) for a in (ln_mlp, w_up, w_down, ln_ple, w_gate, w_proj)]
        + proj_specs,
        out_specs=[tile(d)] + [tile(n) for n in widths],
        out_shape=[jax.ShapeDtypeStruct((m, d), _F32)]
        + [jax.ShapeDtypeStruct((m, n), _BF16) for n in widths],
        compiler_params=_params(1),
        name="post_mlp_ple",
    )(x, y, p, w_out, ln_mlp, w_up, w_down, ln_ple, w_gate, w_proj, *proj_arrays)


def _project_heads(x, ln_ref, w_ref, g_ref, o_ref, *, n_norm, out_scale, segment_major):
    tm = x.shape[0]
    h = _rms(x, ln_ref[...]).astype(_BF16)
    if segment_major:
        r = lax.broadcasted_iota(jnp.int32, (ATTN_BLOCK, ATTN_BLOCK), 0)
        c = lax.broadcasted_iota(jnp.int32, (ATTN_BLOCK, ATTN_BLOCK), 1)
        src = KEY_SEGMENT * (r & (SUBLANES - 1)) + r // SUBLANES
        perm = jnp.where(c == src, 1.0, 0.0).astype(_BF16)
        h = jnp.concatenate(
            [jnp.dot(perm, h[b * ATTN_BLOCK:(b + 1) * ATTN_BLOCK],
                     preferred_element_type=_F32).astype(_BF16)
             for b in range(tm // ATTN_BLOCK)], axis=0)
    first_head = lax.broadcasted_iota(jnp.int32, (tm, LANES), 1) < HEAD_DIM
    gain = g_ref[...] * out_scale
    for c in range(w_ref.shape[1] // MXU_WIDTH):
        y = jnp.dot(h, w_ref[:, c * MXU_WIDTH:(c + 1) * MXU_WIDTH], preferred_element_type=_F32)
        for t in range(MXU_WIDTH // LANES):
            blk = y[:, t * LANES:(t + 1) * LANES]
            col = c * MXU_WIDTH + t * LANES
            if col < n_norm * LANES:
                sq = blk * blk
                s_a = jnp.sum(jnp.where(first_head, sq, 0.0), axis=-1, keepdims=True)
                s_b = jnp.sum(jnp.where(first_head, 0.0, sq), axis=-1, keepdims=True)
                ms = jnp.where(first_head, s_a, s_b) * (1.0 / HEAD_DIM)
                blk = blk * (lax.rsqrt(ms + EPS) * gain)
            o_ref[:, col:col + LANES] = blk.astype(o_ref.dtype)


def _proj_kernel(x_ref, ln_ref, w_ref, g_ref, o_ref, **cfg):
    _project_heads(x_ref[...], ln_ref, w_ref, g_ref, o_ref, **cfg)


def _proj_call(x, layer, ln, w, g_pair, n_norm, out_scale, segment_major=False):
    m, d = x.shape
    n = w.shape[2]
    tm = WIDE_TOKEN_TILE
    return pl.pallas_call(
        functools.partial(_proj_kernel, n_norm=n_norm, out_scale=out_scale,
                          segment_major=segment_major),
        grid=(m // tm,),
        in_specs=[pl.BlockSpec((tm, d), lambda i: (i, 0))]
        + [_layer_spec(a, layer) for a in (ln, w, g_pair)],
        out_specs=pl.BlockSpec((tm, n), lambda i: (i, 0)),
        out_shape=jax.ShapeDtypeStruct((m, n), _BF16),
        compiler_params=_params(1),
        name="head_proj",
    )(x, ln, w, g_pair)


def _attn_kernel(q_ref, k_ref, v_ref, o_ref, qs_ref, passed_ref, acc_ref):
    blk = ATTN_BLOCK
    n_pairs = qs_ref.shape[0]
    n_groups = blk // SUBLANES
    lane = lax.broadcasted_iota(jnp.int32, (blk, LANES), 1)
    first_head = lane < HEAD_DIM
    sub = lax.broadcasted_iota(jnp.int32, (SUBLANES, 2 * blk), 0)

    def block_weights(z, passed, valid):
        beta = 1.0 / (1.0 + jnp.exp2(z))
        keep = 1.0 - beta
        if valid is not None:
            keep = jnp.where(valid, keep, 1.0)
        after = [None] * n_groups
        run = None
        for v in range(n_groups - 1, -1, -1):
            after[v] = run
            kv = keep[v * SUBLANES:(v + 1) * SUBLANES]
            run = kv if run is None else run * kv
        later = jnp.where(sub + 1 < SUBLANES, pltpu.roll(run, SUBLANES - 1, 0), 1.0)
        for step in (1, 2, 4):
            shifted = pltpu.roll(later, SUBLANES - step, 0)
            later = later * jnp.where(sub + step < SUBLANES, shifted, 1.0)
        whole = later * run
        base = later if passed is None else later * passed
        parts = []
        for v in range(n_groups):
            scale = base if after[v] is None else after[v] * base
            parts.append(beta[v * SUBLANES:(v + 1) * SUBLANES] * scale)
        w = jnp.concatenate(parts, axis=0)
        if valid is not None:
            w = jnp.where(valid, w, 0.0)
        total = jnp.broadcast_to(whole[0:1], whole.shape)
        return w.astype(_BF16), total if passed is None else passed * total

    def sweep(j_last, n_blocks, first):
        rows = pl.ds(pl.multiple_of((j_last - (n_blocks - 1)) * blk, blk), n_blocks * blk)
        valid = None
        if first:
            r = lax.broadcasted_iota(jnp.int32, (blk, 2 * blk), 0)
            key = KEY_SEGMENT * (r & (SUBLANES - 1)) + r // SUBLANES
            query = lax.broadcasted_iota(jnp.int32, (blk, 2 * blk), 1) & (blk - 1)
            valid = key < query
        pairs = range(n_pairs)
        zs = [lax.dot_general(k_ref[rows, p * LANES:(p + 1) * LANES], qs_ref[p],
                              (((1,), (1,)), ((), ())), preferred_element_type=_F32)
              for p in pairs]
        weights = []
        for p in pairs:
            passed = None if first else passed_ref[p]
            parts = [None] * n_blocks
            for b in range(n_blocks - 1, -1, -1):
                parts[b], passed = block_weights(
                    zs[p][b * blk:(b + 1) * blk], passed,
                    valid if first and b == n_blocks - 1 else None)
            passed_ref[p] = passed
            weights.append(jnp.concatenate(parts, axis=0) if n_blocks > 1 else parts[0])
        for p in pairs:
            acc = lax.dot_general(v_ref[rows, p * LANES:(p + 1) * LANES], weights[p],
                                  (((0,), (0,)), ((), ())), preferred_element_type=_F32)
            acc_ref[p] = acc if first else acc_ref[p] + acc

    def largest_passed():
        top = passed_ref[0]
        for p in range(1, n_pairs):
            top = jnp.maximum(top, passed_ref[p])
        return jnp.max(top)

    def query_block(i, carry):
        rows = pl.ds(pl.multiple_of(i * blk, blk), blk)
        for p in range(n_pairs):
            q2 = q_ref[rows, p * LANES:(p + 1) * LANES]
            zeros = jnp.zeros_like(q2)
            qs_ref[p] = jnp.concatenate(
                [jnp.where(first_head, q2, zeros), jnp.where(first_head, zeros, q2)], axis=0)

        for n in range(1, STATIC_SWEEP + 1):
            @pl.when((i == n - 1) if n < STATIC_SWEEP else (i >= n - 1))
            def _(n=n):
                sweep(i, n, first=True)

        def cond(state):
            j, top = state
            return jnp.logical_and(j >= 0, top >= WEIGHT_FLOOR)

        def body(state):
            j, _ = state
            sweep(j, 1, first=False)
            return j - 1, largest_passed()

        lax.while_loop(cond, body, (i - STATIC_SWEEP, largest_passed()))
        for p in range(n_pairs):
            acc = acc_ref[p]
            out_t = jnp.concatenate([acc[:HEAD_DIM, :blk], acc[HEAD_DIM:, blk:]], axis=0)
            o_ref[rows, p * LANES:(p + 1) * LANES] = out_t.T.astype(o_ref.dtype)
        return carry

    lax.fori_loop(0, q_ref.shape[0] // blk, query_block, 0)


def _attn_call(q, kv, batch, seq):
    m, d = q.shape
    blk = ATTN_BLOCK
    n_pairs = d // LANES
    return pl.pallas_call(
        _attn_kernel,
        grid=(batch,),
        in_specs=[
            pl.BlockSpec((seq, d), lambda b: (b, 0)),
            pl.BlockSpec((seq, d), lambda b: (b, 0)),
            pl.BlockSpec((seq, d), lambda b: (b, 1)),
        ],
        out_specs=pl.BlockSpec((seq, d), lambda b: (b, 0)),
        out_shape=jax.ShapeDtypeStruct((m, d), _BF16),
        scratch_shapes=[
            pltpu.VMEM((n_pairs, 2 * blk, LANES), _BF16),
            pltpu.VMEM((n_pairs, SUBLANES, 2 * blk), _F32),
            pltpu.VMEM((n_pairs, LANES, 2 * blk), _F32),
        ],
        compiler_params=_params(1),
        name="stick_breaking",
    )(q, kv, kv)


def kernel(x, p, ln_mix_a, w_in_a, g_v_a, w_spatial, b_spatial, w_out_a, ln_kv, w_kv, g_k,
           ln_mix_b, w_q, g_q, w_out_b, ln_mlp, w_up, w_down, ln_ple, w_ple_gate, w_ple_proj):
    batch, seq, d = x.shape
    depth = p.shape[0]
    n_a = ln_mix_a.shape[0]
    m = batch * seq
    xf = x.reshape(m, d)
    pf = p.reshape(depth, m, p.shape[-1])
    bf = lambda w: w.astype(_BF16)
    rows = lambda g: g.reshape(g.shape[0], 1, g.shape[1])
    pairs = lambda g: jnp.concatenate([g, g], axis=-1).reshape(g.shape[0], 1, LANES)
    ln_mix_a, g_v_a, ln_mix_b, ln_mlp, ln_ple = map(rows, (ln_mix_a, g_v_a, ln_mix_b, ln_mlp, ln_ple))
    w_in_a, w_out_a, w_q, w_out_b, w_up, w_down, w_ple_gate, w_ple_proj = map(
        bf, (w_in_a, w_out_a, w_q, w_out_b, w_up, w_down, w_ple_gate, w_ple_proj))
    bias = jnp.repeat(jnp.swapaxes(b_spatial, 1, 2), d // b_spatial.shape[1], axis=2)
    q_proj = lambda j: dict(layer=j, ln=ln_mix_b, w=w_q, gain=pairs(g_q), n_norm=d // LANES,
                            out_scale=-LOG2_E * HEAD_DIM ** -0.5, segment_major=False)
    kv_proj = dict(layer=0, ln=rows(ln_kv[None]), w=bf(w_kv)[None], gain=pairs(g_k[None]),
                   n_norm=d // LANES, out_scale=1.0, segment_major=True)
    kv_shared = q = None
    for i in range(depth):
        if i < n_a:
            y = _sgu_call(xf, i, ln_mix_a, w_in_a, g_v_a, w_spatial, bias)
            mixer_layer, w_o = i, w_out_a
        else:
            j = i - n_a
            if q is None:
                hp = q_proj(j)
                q = _proj_call(xf, hp["layer"], hp["ln"], hp["w"], hp["gain"], hp["n_norm"],
                               hp["out_scale"])
            y = _attn_call(q, kv_shared, batch, seq)
            mixer_layer, w_o = j, w_out_b
        head_projs = ([kv_proj] if i == n_a - 1 else []) + (
            [q_proj(i + 1 - n_a)] if n_a <= i + 1 < depth else [])
        xf, *projected = _post_call(xf, y, i, pf, mixer_layer, w_o, ln_mlp, w_up, w_down, ln_ple,
                                    w_ple_gate, w_ple_proj, head_projs)
        if i == n_a - 1:
            kv_shared = projected.pop(0)
        q = projected.pop(0) if projected else None
    return xf.reshape(batch, seq, d)
```

```python
import functools

import jax
import jax.numpy as jnp
from jax import lax
from jax.experimental import pallas as pl
from jax.experimental.pallas import tpu as pltpu

EPS = 1e-6
CHUNK = 128
HEAD_DIM = 64
LANES = 128
MXU_WIDTH = 256
BF16_ROWS = 16
TOKEN_TILE = 512
WIDE_TOKEN_TILE = 1024
ATTN_BLOCK = 128
STATIC_SWEEP = 3
FF_CHUNK = 1024
VMEM_LIMIT = 56 * 1024 * 1024
SUBLANES = 8
KEY_SEGMENT = ATTN_BLOCK // SUBLANES
LOG2_E = 1.4426950408889634
WEIGHT_FLOOR = 2.0 ** -126

_BF16 = jnp.bfloat16
_F32 = jnp.float32


def _rms(x, g):
    return x * lax.rsqrt(jnp.mean(x * x, axis=-1, keepdims=True) + EPS) * g


def _gelu_tanh(x):
    c = -2.0 * LOG2_E * (2.0 / jnp.pi) ** 0.5
    return x / (1.0 + jnp.exp2(x * (x * x * (c * 0.044715) + c)))


def _layer_spec(param):
    stacked, layer = param
    rest = stacked.shape[1:]
    return pl.BlockSpec((None,) + rest, lambda *_: (layer,) + (0,) * len(rest),
                        pipeline_mode=pl.Buffered(1))


def _rider_specs(riders, n_steps):
    arrays, in_specs, out_specs, out_shapes = [], [], [], []
    for stacked, layer in riders:
        _, r, c = stacked.shape
        rb = r // n_steps
        assert rb * n_steps == r and rb % BF16_ROWS == 0, (stacked.shape, n_steps)
        arrays.append(stacked)
        in_specs.append(pl.BlockSpec((None, rb, c), lambda i, layer=layer: (layer, i, 0)))
        out_specs.append(pl.BlockSpec((rb, c), lambda i: (i, 0)))
        out_shapes.append(jax.ShapeDtypeStruct((r, c), _BF16))
    return arrays, in_specs, out_specs, out_shapes


def _cast_riders(src_refs, dst_refs):
    for src, dst in zip(src_refs, dst_refs):
        dst[...] = src[...].astype(dst.dtype)


def _params(n_axes):
    return pltpu.CompilerParams(
        dimension_semantics=("arbitrary",) * n_axes, vmem_limit_bytes=VMEM_LIMIT)


def _sgu_kernel(x_ref, ln_ref, win_ref, gv_ref, ws_ref, bias_ref, *rest, n_riders):
    y_ref = rest[n_riders]
    _cast_riders(rest[:n_riders], rest[n_riders + 1:])
    tm, d = x_ref.shape
    n_groups = ws_ref.shape[0]
    n_chunks = tm // CHUNK
    h = _rms(x_ref[...], ln_ref[...]).astype(_BF16)
    v = _gelu_tanh(jnp.dot(h, win_ref[:, d:].astype(_BF16), preferred_element_type=_F32))
    v = _rms(v, gv_ref[...]).astype(_BF16)
    u = _gelu_tanh(jnp.dot(h, win_ref[:, :d].astype(_BF16), preferred_element_type=_F32))
    row = lax.broadcasted_iota(jnp.int32, (CHUNK, CHUNK), 0)
    col = lax.broadcasted_iota(jnp.int32, (CHUNK, CHUNK), 1)
    causal = col <= row
    mix_cols = []
    for g in range(n_groups):
        w = jnp.where(causal, ws_ref[g], 0.0).astype(_BF16)
        vg = jnp.concatenate(
            [v[c * CHUNK:(c + 1) * CHUNK, g * LANES:(g + 1) * LANES] for c in range(n_chunks)],
            axis=1)
        mg = jnp.dot(w, vg, preferred_element_type=_F32)
        mix_cols.append(jnp.concatenate(
            [mg[:, c * LANES:(c + 1) * LANES] for c in range(n_chunks)], axis=0))
    mix = jnp.concatenate(mix_cols, axis=1)
    bias = jnp.concatenate([bias_ref[...]] * n_chunks, axis=0)
    y_ref[...] = (u * (mix + bias)).astype(y_ref.dtype)


def _sgu_call(x, params, riders=()):
    m, d = x.shape
    tm = WIDE_TOKEN_TILE
    tile = pl.BlockSpec((tm, d), lambda i: (i, 0))
    r_arrays, r_in, r_out, r_shapes = _rider_specs(riders, m // tm)
    return pl.pallas_call(
        functools.partial(_sgu_kernel, n_riders=len(riders)),
        grid=(m // tm,),
        in_specs=[tile] + [_layer_spec(a) for a in params] + r_in,
        out_specs=[tile] + r_out,
        out_shape=[jax.ShapeDtypeStruct((m, d), _BF16)] + r_shapes,
        compiler_params=_params(1),
        name="sgu_mixer",
    )(x, *[a for a, _ in params], *r_arrays)


def _post_kernel(x_ref, y_ref, p_ref, wo_ref, lnm_ref, wup_ref, wdn_ref,
                 lnp_ref, wg_ref, wp_ref, *rest, n_riders):
    o_ref = rest[n_riders]
    _cast_riders(rest[:n_riders], rest[n_riders + 1:])
    d_ff = wup_ref.shape[1]
    x = x_ref[...] + jnp.dot(y_ref[...], wo_ref[...], preferred_element_type=_F32)
    h = _rms(x, lnm_ref[...]).astype(_BF16)
    acc = x
    for c in range(d_ff // FF_CHUNK):
        a = jnp.dot(h, wup_ref[:, c * FF_CHUNK:(c + 1) * FF_CHUNK], preferred_element_type=_F32)
        a = jnp.maximum(a, 0.0)
        a = (a * a).astype(_BF16)
        acc = acc + jnp.dot(a, wdn_ref[c * FF_CHUNK:(c + 1) * FF_CHUNK, :],
                            preferred_element_type=_F32)
    x = acc
    hg = _rms(x, lnp_ref[...]).astype(_BF16)
    gate = 1.0 / (1.0 + jnp.exp(-jnp.dot(hg, wg_ref[...], preferred_element_type=_F32)))
    pp = jnp.dot(p_ref[...].astype(_BF16), wp_ref[...], preferred_element_type=_F32)
    o_ref[...] = x + pp * gate


def _post_call(x, y, p_param, params, riders=()):
    m, d = x.shape
    tm = TOKEN_TILE
    tile = pl.BlockSpec((tm, d), lambda i: (i, 0))
    p, p_layer = p_param
    r_arrays, r_in, r_out, r_shapes = _rider_specs(riders, m // tm)
    return pl.pallas_call(
        functools.partial(_post_kernel, n_riders=len(riders)),
        grid=(m // tm,),
        in_specs=[tile, tile, pl.BlockSpec((None, tm, p.shape[2]), lambda i: (p_layer, i, 0))]
        + [_layer_spec(a) for a in params] + r_in,
        out_specs=[tile] + r_out,
        out_shape=[jax.ShapeDtypeStruct((m, d), _F32)] + r_shapes,
        compiler_params=_params(1),
        name="post_mlp_ple",
    )(x, y, p, *[a for a, _ in params], *r_arrays)


def _proj_kernel(x_ref, ln_ref, w_ref, g_ref, o_ref, *, n_norm, out_scale, segment_major):
    tm = x_ref.shape[0]
    h = _rms(x_ref[...], ln_ref[...]).astype(_BF16)
    if segment_major:
        r = lax.broadcasted_iota(jnp.int32, (ATTN_BLOCK, ATTN_BLOCK), 0)
        c = lax.broadcasted_iota(jnp.int32, (ATTN_BLOCK, ATTN_BLOCK), 1)
        src = KEY_SEGMENT * (r & (SUBLANES - 1)) + r // SUBLANES
        perm = jnp.where(c == src, 1.0, 0.0).astype(_BF16)
        h = jnp.concatenate(
            [jnp.dot(perm, h[b * ATTN_BLOCK:(b + 1) * ATTN_BLOCK],
                     preferred_element_type=_F32).astype(_BF16)
             for b in range(tm // ATTN_BLOCK)], axis=0)
    first_head = lax.broadcasted_iota(jnp.int32, (tm, LANES), 1) < HEAD_DIM
    gain = g_ref[...] * out_scale
    for c in range(w_ref.shape[1] // MXU_WIDTH):
        y = jnp.dot(h, w_ref[:, c * MXU_WIDTH:(c + 1) * MXU_WIDTH], preferred_element_type=_F32)
        for t in range(MXU_WIDTH // LANES):
            blk = y[:, t * LANES:(t + 1) * LANES]
            col = c * MXU_WIDTH + t * LANES
            if col < n_norm * LANES:
                sq = blk * blk
                s_a = jnp.sum(jnp.where(first_head, sq, 0.0), axis=-1, keepdims=True)
                s_b = jnp.sum(jnp.where(first_head, 0.0, sq), axis=-1, keepdims=True)
                ms = jnp.where(first_head, s_a, s_b) * (1.0 / HEAD_DIM)
                blk = blk * (lax.rsqrt(ms + EPS) * gain)
            o_ref[:, col:col + LANES] = blk.astype(o_ref.dtype)


def _proj_call(x, params, n_norm, out_scale, segment_major=False):
    m, d = x.shape
    n = params[1][0].shape[2]
    tm = WIDE_TOKEN_TILE
    return pl.pallas_call(
        functools.partial(_proj_kernel, n_norm=n_norm, out_scale=out_scale,
                          segment_major=segment_major),
        grid=(m // tm,),
        in_specs=[pl.BlockSpec((tm, d), lambda i: (i, 0))] + [_layer_spec(a) for a in params],
        out_specs=pl.BlockSpec((tm, n), lambda i: (i, 0)),
        out_shape=jax.ShapeDtypeStruct((m, n), _BF16),
        compiler_params=_params(1),
        name="head_proj",
    )(x, *[a for a, _ in params])


def _attn_kernel(q_ref, k_ref, v_ref, o_ref, qs_ref, passed_ref, acc_ref):
    blk = ATTN_BLOCK
    n_pairs = qs_ref.shape[0]
    n_groups = blk // SUBLANES
    lane = lax.broadcasted_iota(jnp.int32, (blk, LANES), 1)
    first_head = lane < HEAD_DIM
    sub = lax.broadcasted_iota(jnp.int32, (SUBLANES, 2 * blk), 0)

    def block_weights(z, passed, valid):
        beta = 1.0 / (1.0 + jnp.exp2(z))
        keep = 1.0 - beta
        if valid is not None:
            keep = jnp.where(valid, keep, 1.0)
        after = [None] * n_groups
        run = None
        for v in range(n_groups - 1, -1, -1):
            after[v] = run
            kv = keep[v * SUBLANES:(v + 1) * SUBLANES]
            run = kv if run is None else run * kv
        later = jnp.where(sub + 1 < SUBLANES, pltpu.roll(run, SUBLANES - 1, 0), 1.0)
        for step in (1, 2, 4):
            shifted = pltpu.roll(later, SUBLANES - step, 0)
            later = later * jnp.where(sub + step < SUBLANES, shifted, 1.0)
        whole = later * run
        base = later if passed is None else later * passed
        parts = []
        for v in range(n_groups):
            scale = base if after[v] is None else after[v] * base
            parts.append(beta[v * SUBLANES:(v + 1) * SUBLANES] * scale)
        w = jnp.concatenate(parts, axis=0)
        if valid is not None:
            w = jnp.where(valid, w, 0.0)
        total = jnp.broadcast_to(whole[0:1], whole.shape)
        return w.astype(_BF16), total if passed is None else passed * total

    def sweep(j_last, n_blocks, first):
        rows = pl.ds(pl.multiple_of((j_last - (n_blocks - 1)) * blk, blk), n_blocks * blk)
        valid = None
        if first:
            r = lax.broadcasted_iota(jnp.int32, (blk, 2 * blk), 0)
            key = KEY_SEGMENT * (r & (SUBLANES - 1)) + r // SUBLANES
            query = lax.broadcasted_iota(jnp.int32, (blk, 2 * blk), 1) & (blk - 1)
            valid = key < query
        pairs = range(n_pairs)
        zs = [lax.dot_general(k_ref[rows, p * LANES:(p + 1) * LANES], qs_ref[p],
                              (((1,), (1,)), ((), ())), preferred_element_type=_F32)
              for p in pairs]
        weights = []
        for p in pairs:
            passed = None if first else passed_ref[p]
            parts = [None] * n_blocks
            for b in range(n_blocks - 1, -1, -1):
                parts[b], passed = block_weights(
                    zs[p][b * blk:(b + 1) * blk], passed,
                    valid if first and b == n_blocks - 1 else None)
            passed_ref[p] = passed
            weights.append(jnp.concatenate(parts, axis=0) if n_blocks > 1 else parts[0])
        for p in pairs:
            acc = lax.dot_general(v_ref[rows, p * LANES:(p + 1) * LANES], weights[p],
                                  (((0,), (0,)), ((), ())), preferred_element_type=_F32)
            acc_ref[p] = acc if first else acc_ref[p] + acc

    def largest_passed():
        top = passed_ref[0]
        for p in range(1, n_pairs):
            top = jnp.maximum(top, passed_ref[p])
        return jnp.max(top)

    def query_block(i, carry):
        rows = pl.ds(pl.multiple_of(i * blk, blk), blk)
        for p in range(n_pairs):
            q2 = q_ref[rows, p * LANES:(p + 1) * LANES]
            zeros = jnp.zeros_like(q2)
            qs_ref[p] = jnp.concatenate(
                [jnp.where(first_head, q2, zeros), jnp.where(first_head, zeros, q2)], axis=0)

        for n in range(1, STATIC_SWEEP + 1):
            @pl.when((i == n - 1) if n < STATIC_SWEEP else (i >= n - 1))
            def _(n=n):
                sweep(i, n, first=True)

        def cond(state):
            j, top = state
            return jnp.logical_and(j >= 0, top >= WEIGHT_FLOOR)

        def body(state):
            j, _ = state
            sweep(j, 1, first=False)
            return j - 1, largest_passed()

        lax.while_loop(cond, body, (i - STATIC_SWEEP, largest_passed()))
        for p in range(n_pairs):
            acc = acc_ref[p]
            out_t = jnp.concatenate([acc[:HEAD_DIM, :blk], acc[HEAD_DIM:, blk:]], axis=0)
            o_ref[rows, p * LANES:(p + 1) * LANES] = out_t.T.astype(o_ref.dtype)
        return carry

    lax.fori_loop(0, q_ref.shape[0] // blk, query_block, 0)


def _attn_call(q, kv, batch, seq):
    m, d = q.shape
    blk = ATTN_BLOCK
    n_pairs = d // LANES
    return pl.pallas_call(
        _attn_kernel,
        grid=(batch,),
        in_specs=[
            pl.BlockSpec((seq, d), lambda b: (b, 0)),
            pl.BlockSpec((seq, d), lambda b: (b, 0)),
            pl.BlockSpec((seq, d), lambda b: (b, 1)),
        ],
        out_specs=pl.BlockSpec((seq, d), lambda b: (b, 0)),
        out_shape=jax.ShapeDtypeStruct((m, d), _BF16),
        scratch_shapes=[
            pltpu.VMEM((n_pairs, 2 * blk, LANES), _BF16),
            pltpu.VMEM((n_pairs, SUBLANES, 2 * blk), _F32),
            pltpu.VMEM((n_pairs, LANES, 2 * blk), _F32),
        ],
        compiler_params=_params(1),
        name="stick_breaking",
    )(q, kv, kv)


def kernel(x, p, ln_mix_a, w_in_a, g_v_a, w_spatial, b_spatial, w_out_a, ln_kv, w_kv, g_k,
           ln_mix_b, w_q, g_q, w_out_b, ln_mlp, w_up, w_down, ln_ple, w_ple_gate, w_ple_proj):
    batch, seq, d = x.shape
    depth = p.shape[0]
    n_a = ln_mix_a.shape[0]
    m = batch * seq
    xf = x.reshape(m, d)
    pf = p.reshape(depth, m, p.shape[-1])
    rows = lambda g: g.reshape(g.shape[0], 1, g.shape[1])
    pairs = lambda g: jnp.concatenate([g, g], axis=-1).reshape(g.shape[0], 1, LANES)
    ln_mix_a, g_v_a, ln_mix_b, ln_mlp, ln_ple = map(rows, (ln_mix_a, g_v_a, ln_mix_b, ln_mlp, ln_ple))
    w_ple_proj = w_ple_proj.astype(_BF16)
    bias = jnp.repeat(jnp.swapaxes(b_spatial, 1, 2), d // b_spatial.shape[1], axis=2)

    def post_weights(i):
        out = (w_out_a, i) if i < n_a else (w_out_b, i - n_a)
        return {("out", i): out, ("up", i): (w_up, i), ("down", i): (w_down, i),
                ("gate", i): (w_ple_gate, i)}

    cast = {}

    def bf16_weight(name, source):
        if name in cast:
            return cast.pop(name)[None], 0
        stacked, layer = source
        return stacked.astype(_BF16), layer

    def ride(call, wanted):
        names = list(wanted)
        main, *casts = call([wanted[n] for n in names])
        cast.update(zip(names, casts))
        return main

    kv_shared = None
    for i in range(depth):
        if i < n_a:
            mixer_params = [(ln_mix_a, i), (w_in_a, i), (g_v_a, i), (w_spatial, i), (bias, i)]
            wanted = post_weights(i) if i == 0 else {}
            y = ride(lambda riders: _sgu_call(xf, mixer_params, riders), wanted)
        else:
            j = i - n_a
            q_params = [(ln_mix_b, j), bf16_weight(("q", j), (w_q, j)), (pairs(g_q), j)]
            q = _proj_call(xf, q_params, d // LANES, -LOG2_E * HEAD_DIM ** -0.5)
            y = _attn_call(q, kv_shared, batch, seq)
        weights = {n: bf16_weight(n, src) for n, src in post_weights(i).items()}
        post_params = [weights[("out", i)], (ln_mlp, i), weights[("up", i)], weights[("down", i)],
                       (ln_ple, i), weights[("gate", i)], (w_ple_proj, i)]
        wanted = post_weights(i + 1) if i + 1 < depth else {}
        if i == n_a - 1:
            wanted[("kv", 0)] = (w_kv[None], 0)
        if n_a <= i + 1 < depth:
            wanted[("q", i + 1 - n_a)] = (w_q, i + 1 - n_a)
        xf = ride(lambda riders: _post_call(xf, y, (pf, i), post_params, riders), wanted)
        if i == n_a - 1:
            kv_params = [(rows(ln_kv[None]), 0), bf16_weight(("kv", 0), (w_kv[None], 0)),
                         (pairs(g_k[None]), 0)]
            kv_shared = _proj_call(xf, kv_params, d // LANES, 1.0, segment_major=True)
    return xf.reshape(batch, seq, d)
```

```python
import functools

import jax
import jax.numpy as jnp
from jax import lax
from jax.experimental import pallas as pl
from jax.experimental.pallas import tpu as pltpu

EPS = 1e-6
CHUNK = 128
HEAD_DIM = 64
LANES = 128
MXU_WIDTH = 256
BF16_ROWS = 16
TOKEN_TILE = 512
WIDE_TOKEN_TILE = 1024
ATTN_BLOCK = 128
STATIC_SWEEP = 3
FF_CHUNK = 1024
VMEM_LIMIT = 56 * 1024 * 1024
SUBLANES = 8
KEY_SEGMENT = ATTN_BLOCK // SUBLANES
LOG2_E = 1.4426950408889634
WEIGHT_FLOOR = 2.0 ** -126

_BF16 = jnp.bfloat16
_F32 = jnp.float32


def _rms(x, g):
    return x * lax.rsqrt(jnp.mean(x * x, axis=-1, keepdims=True) + EPS) * g


def _gelu_tanh(x):
    c = -2.0 * LOG2_E * (2.0 / jnp.pi) ** 0.5
    return x / (1.0 + jnp.exp2(x * (x * x * (c * 0.044715) + c)))


def _layer_spec(param):
    stacked, layer = param
    rest = stacked.shape[1:]
    return pl.BlockSpec((None,) + rest, lambda *_: (layer,) + (0,) * len(rest),
                        pipeline_mode=pl.Buffered(1))


def _rider_specs(riders, n_steps):
    arrays, in_specs, out_specs, out_shapes = [], [], [], []
    for stacked, layer in riders:
        _, r, c = stacked.shape
        rb = r // n_steps
        assert rb * n_steps == r and rb % BF16_ROWS == 0, (stacked.shape, n_steps)
        arrays.append(stacked)
        in_specs.append(pl.BlockSpec((None, rb, c), lambda i, layer=layer: (layer, i, 0)))
        out_specs.append(pl.BlockSpec((rb, c), lambda i: (i, 0)))
        out_shapes.append(jax.ShapeDtypeStruct((r, c), _BF16))
    return arrays, in_specs, out_specs, out_shapes


def _cast_riders(src_refs, dst_refs):
    for src, dst in zip(src_refs, dst_refs):
        dst[...] = src[...].astype(dst.dtype)


def _params(n_axes):
    return pltpu.CompilerParams(
        dimension_semantics=("arbitrary",) * n_axes, vmem_limit_bytes=VMEM_LIMIT)


def _sgu_kernel(x_ref, ln_ref, win_ref, gv_ref, ws_ref, bias_ref, *rest, n_riders):
    y_ref = rest[n_riders]
    _cast_riders(rest[:n_riders], rest[n_riders + 1:])
    tm, d = x_ref.shape
    n_groups = ws_ref.shape[0]
    n_chunks = tm // CHUNK
    h = _rms(x_ref[...], ln_ref[...]).astype(_BF16)
    v = _gelu_tanh(jnp.dot(h, win_ref[:, d:].astype(_BF16), preferred_element_type=_F32))
    v = _rms(v, gv_ref[...]).astype(_BF16)
    u = _gelu_tanh(jnp.dot(h, win_ref[:, :d].astype(_BF16), preferred_element_type=_F32))
    row = lax.broadcasted_iota(jnp.int32, (CHUNK, CHUNK), 0)
    col = lax.broadcasted_iota(jnp.int32, (CHUNK, CHUNK), 1)
    causal = col <= row
    mix_cols = []
    for g in range(n_groups):
        w = jnp.where(causal, ws_ref[g], 0.0).astype(_BF16)
        vg = jnp.concatenate(
            [v[c * CHUNK:(c + 1) * CHUNK, g * LANES:(g + 1) * LANES] for c in range(n_chunks)],
            axis=1)
        mg = jnp.dot(w, vg, preferred_element_type=_F32)
        mix_cols.append(jnp.concatenate(
            [mg[:, c * LANES:(c + 1) * LANES] for c in range(n_chunks)], axis=0))
    mix = jnp.concatenate(mix_cols, axis=1)
    bias = jnp.concatenate([bias_ref[...]] * n_chunks, axis=0)
    y_ref[...] = (u * (mix + bias)).astype(y_ref.dtype)


def _sgu_call(x, params, riders=()):
    m, d = x.shape
    tm = WIDE_TOKEN_TILE
    tile = pl.BlockSpec((tm, d), lambda i: (i, 0))
    r_arrays, r_in, r_out, r_shapes = _rider_specs(riders, m // tm)
    return pl.pallas_call(
        functools.partial(_sgu_kernel, n_riders=len(riders)),
        grid=(m // tm,),
        in_specs=[tile] + [_layer_spec(a) for a in params] + r_in,
        out_specs=[tile] + r_out,
        out_shape=[jax.ShapeDtypeStruct((m, d), _BF16)] + r_shapes,
        compiler_params=_params(1),
        name="sgu_mixer",
    )(x, *[a for a, _ in params], *r_arrays)


def _post_kernel(x_ref, y_ref, p_ref, wo_ref, lnm_ref, wup_ref, wdn_ref,
                 lnp_ref, wg_ref, wp_ref, *rest, n_riders):
    o_ref = rest[n_riders]
    _cast_riders(rest[:n_riders], rest[n_riders + 1:])
    d_ff = wup_ref.shape[1]
    x = x_ref[...] + jnp.dot(y_ref[...], wo_ref[...], preferred_element_type=_F32)
    h = _rms(x, lnm_ref[...]).astype(_BF16)
    acc = x
    for c in range(d_ff // FF_CHUNK):
        a = jnp.dot(h, wup_ref[:, c * FF_CHUNK:(c + 1) * FF_CHUNK], preferred_element_type=_F32)
        a = jnp.maximum(a, 0.0)
        a = (a * a).astype(_BF16)
        acc = acc + jnp.dot(a, wdn_ref[c * FF_CHUNK:(c + 1) * FF_CHUNK, :],
                            preferred_element_type=_F32)
    x = acc
    hg = _rms(x, lnp_ref[...]).astype(_BF16)
    gate = 1.0 / (1.0 + jnp.exp(-jnp.dot(hg, wg_ref[...], preferred_element_type=_F32)))
    pp = jnp.dot(p_ref[...].astype(_BF16), wp_ref[...], preferred_element_type=_F32)
    o_ref[...] = x + pp * gate


def _post_call(x, y, p_param, params, riders=()):
    m, d = x.shape
    tm = TOKEN_TILE
    tile = pl.BlockSpec((tm, d), lambda i: (i, 0))
    p, p_layer = p_param
    r_arrays, r_in, r_out, r_shapes = _rider_specs(riders, m // tm)
    return pl.pallas_call(
        functools.partial(_post_kernel, n_riders=len(riders)),
        grid=(m // tm,),
        in_specs=[tile, tile, pl.BlockSpec((None, tm, p.shape[2]), lambda i: (p_layer, i, 0))]
        + [_layer_spec(a) for a in params] + r_in,
        out_specs=[tile] + r_out,
        out_shape=[jax.ShapeDtypeStruct((m, d), _F32)] + r_shapes,
        compiler_params=_params(1),
        name="post_mlp_ple",
    )(x, y, p, *[a for a, _ in params], *r_arrays)


def _proj_kernel(x_ref, *refs, groups):
    x = x_ref[...]
    for n, cfg in enumerate(groups):
        _project_heads(x, *refs[3 * n:3 * n + 3], refs[3 * len(groups) + n], **cfg)


def _project_heads(x, ln_ref, w_ref, g_ref, o_ref, *, n_norm, out_scale, segment_major):
    tm = x.shape[0]
    h = _rms(x, ln_ref[...]).astype(_BF16)
    if segment_major:
        r = lax.broadcasted_iota(jnp.int32, (ATTN_BLOCK, ATTN_BLOCK), 0)
        c = lax.broadcasted_iota(jnp.int32, (ATTN_BLOCK, ATTN_BLOCK), 1)
        src = KEY_SEGMENT * (r & (SUBLANES - 1)) + r // SUBLANES
        perm = jnp.where(c == src, 1.0, 0.0).astype(_BF16)
        h = jnp.concatenate(
            [jnp.dot(perm, h[b * ATTN_BLOCK:(b + 1) * ATTN_BLOCK],
                     preferred_element_type=_F32).astype(_BF16)
             for b in range(tm // ATTN_BLOCK)], axis=0)
    first_head = lax.broadcasted_iota(jnp.int32, (tm, LANES), 1) < HEAD_DIM
    gain = g_ref[...] * out_scale
    for c in range(w_ref.shape[1] // MXU_WIDTH):
        y = jnp.dot(h, w_ref[:, c * MXU_WIDTH:(c + 1) * MXU_WIDTH], preferred_element_type=_F32)
        for t in range(MXU_WIDTH // LANES):
            blk = y[:, t * LANES:(t + 1) * LANES]
            col = c * MXU_WIDTH + t * LANES
            if col < n_norm * LANES:
                sq = blk * blk
                s_a = jnp.sum(jnp.where(first_head, sq, 0.0), axis=-1, keepdims=True)
                s_b = jnp.sum(jnp.where(first_head, 0.0, sq), axis=-1, keepdims=True)
                ms = jnp.where(first_head, s_a, s_b) * (1.0 / HEAD_DIM)
                blk = blk * (lax.rsqrt(ms + EPS) * gain)
            o_ref[:, col:col + LANES] = blk.astype(o_ref.dtype)


def _proj_call(x, groups):
    m, d = x.shape
    tm = WIDE_TOKEN_TILE
    tile = lambda width: pl.BlockSpec((tm, width), lambda i: (i, 0))
    params = [a for g in groups for a in g["params"]]
    widths = [g["params"][1][0].shape[2] for g in groups]
    cfgs = tuple({k: g[k] for k in ("n_norm", "out_scale", "segment_major")} for g in groups)
    return pl.pallas_call(
        functools.partial(_proj_kernel, groups=cfgs),
        grid=(m // tm,),
        in_specs=[tile(d)] + [_layer_spec(a) for a in params],
        out_specs=[tile(n) for n in widths],
        out_shape=[jax.ShapeDtypeStruct((m, n), _BF16) for n in widths],
        compiler_params=_params(1),
        name="head_proj",
    )(x, *[a for a, _ in params])


def _attn_kernel(q_ref, k_ref, v_ref, o_ref, qs_ref, passed_ref, acc_ref):
    blk = ATTN_BLOCK
    n_pairs = qs_ref.shape[0]
    n_groups = blk // SUBLANES
    lane = lax.broadcasted_iota(jnp.int32, (blk, LANES), 1)
    first_head = lane < HEAD_DIM
    sub = lax.broadcasted_iota(jnp.int32, (SUBLANES, 2 * blk), 0)

    def block_weights(z, passed, valid):
        beta = 1.0 / (1.0 + jnp.exp2(z))
        keep = 1.0 - beta
        if valid is not None:
            keep = jnp.where(valid, keep, 1.0)
        after = [None] * n_groups
        run = None
        for v in range(n_groups - 1, -1, -1):
            after[v] = run
            kv = keep[v * SUBLANES:(v + 1) * SUBLANES]
            run = kv if run is None else run * kv
        later = jnp.where(sub + 1 < SUBLANES, pltpu.roll(run, SUBLANES - 1, 0), 1.0)
        for step in (1, 2, 4):
            shifted = pltpu.roll(later, SUBLANES - step, 0)
            later = later * jnp.where(sub + step < SUBLANES, shifted, 1.0)
        whole = later * run
        base = later if passed is None else later * passed
        parts = []
        for v in range(n_groups):
            scale = base if after[v] is None else after[v] * base
            parts.append(beta[v * SUBLANES:(v + 1) * SUBLANES] * scale)
        w = jnp.concatenate(parts, axis=0)
        if valid is not None:
            w = jnp.where(valid, w, 0.0)
        total = jnp.broadcast_to(whole[0:1], whole.shape)
        return w.astype(_BF16), total if passed is None else passed * total

    def sweep(j_last, n_blocks, first):
        rows = pl.ds(pl.multiple_of((j_last - (n_blocks - 1)) * blk, blk), n_blocks * blk)
        valid = None
        if first:
            r = lax.broadcasted_iota(jnp.int32, (blk, 2 * blk), 0)
            key = KEY_SEGMENT * (r & (SUBLANES - 1)) + r // SUBLANES
            query = lax.broadcasted_iota(jnp.int32, (blk, 2 * blk), 1) & (blk - 1)
            valid = key < query
        pairs = range(n_pairs)
        zs = [lax.dot_general(k_ref[rows, p * LANES:(p + 1) * LANES], qs_ref[p],
                              (((1,), (1,)), ((), ())), preferred_element_type=_F32)
              for p in pairs]
        weights = []
        for p in pairs:
            passed = None if first else passed_ref[p]
            parts = [None] * n_blocks
            for b in range(n_blocks - 1, -1, -1):
                parts[b], passed = block_weights(
                    zs[p][b * blk:(b + 1) * blk], passed,
                    valid if first and b == n_blocks - 1 else None)
            passed_ref[p] = passed
            weights.append(jnp.concatenate(parts, axis=0) if n_blocks > 1 else parts[0])
        for p in pairs:
            acc = lax.dot_general(v_ref[rows, p * LANES:(p + 1) * LANES], weights[p],
                                  (((0,), (0,)), ((), ())), preferred_element_type=_F32)
            acc_ref[p] = acc if first else acc_ref[p] + acc

    def largest_passed():
        top = passed_ref[0]
        for p in range(1, n_pairs):
            top = jnp.maximum(top, passed_ref[p])
        return jnp.max(top)

    def query_block(i, carry):
        rows = pl.ds(pl.multiple_of(i * blk, blk), blk)
        for p in range(n_pairs):
            q2 = q_ref[rows, p * LANES:(p + 1) * LANES]
            zeros = jnp.zeros_like(q2)
            qs_ref[p] = jnp.concatenate(
                [jnp.where(first_head, q2, zeros), jnp.where(first_head, zeros, q2)], axis=0)

        for n in range(1, STATIC_SWEEP + 1):
            @pl.when((i == n - 1) if n < STATIC_SWEEP else (i >= n - 1))
            def _(n=n):
                sweep(i, n, first=True)

        def cond(state):
            j, top = state
            return jnp.logical_and(j >= 0, top >= WEIGHT_FLOOR)

        def body(state):
            j, _ = state
            sweep(j, 1, first=False)
            return j - 1, largest_passed()

        lax.while_loop(cond, body, (i - STATIC_SWEEP, largest_passed()))
        for p in range(n_pairs):
            acc = acc_ref[p]
            out_t = jnp.concatenate([acc[:HEAD_DIM, :blk], acc[HEAD_DIM:, blk:]], axis=0)
            o_ref[rows, p * LANES:(p + 1) * LANES] = out_t.T.astype(o_ref.dtype)
        return carry

    lax.fori_loop(0, q_ref.shape[0] // blk, query_block, 0)


def _attn_call(q, kv, batch, seq):
    m, d = q.shape
    blk = ATTN_BLOCK
    n_pairs = d // LANES
    return pl.pallas_call(
        _attn_kernel,
        grid=(batch,),
        in_specs=[
            pl.BlockSpec((seq, d), lambda b: (b, 0)),
            pl.BlockSpec((seq, d), lambda b: (b, 0)),
            pl.BlockSpec((seq, d), lambda b: (b, 1)),
        ],
        out_specs=pl.BlockSpec((seq, d), lambda b: (b, 0)),
        out_shape=jax.ShapeDtypeStruct((m, d), _BF16),
        scratch_shapes=[
            pltpu.VMEM((n_pairs, 2 * blk, LANES), _BF16),
            pltpu.VMEM((n_pairs, SUBLANES, 2 * blk), _F32),
            pltpu.VMEM((n_pairs, LANES, 2 * blk), _F32),
        ],
        compiler_params=_params(1),
        name="stick_breaking",
    )(q, kv, kv)


def kernel(x, p, ln_mix_a, w_in_a, g_v_a, w_spatial, b_spatial, w_out_a, ln_kv, w_kv, g_k,
           ln_mix_b, w_q, g_q, w_out_b, ln_mlp, w_up, w_down, ln_ple, w_ple_gate, w_ple_proj):
    batch, seq, d = x.shape
    depth = p.shape[0]
    n_a = ln_mix_a.shape[0]
    m = batch * seq
    xf = x.reshape(m, d)
    pf = p.reshape(depth, m, p.shape[-1])
    rows = lambda g: g.reshape(g.shape[0], 1, g.shape[1])
    pairs = lambda g: jnp.concatenate([g, g], axis=-1).reshape(g.shape[0], 1, LANES)
    ln_mix_a, g_v_a, ln_mix_b, ln_mlp, ln_ple = map(rows, (ln_mix_a, g_v_a, ln_mix_b, ln_mlp, ln_ple))
    w_ple_proj = w_ple_proj.astype(_BF16)
    bias = jnp.repeat(jnp.swapaxes(b_spatial, 1, 2), d // b_spatial.shape[1], axis=2)

    def post_weights(i):
        out = (w_out_a, i) if i < n_a else (w_out_b, i - n_a)
        return {("out", i): out, ("up", i): (w_up, i), ("down", i): (w_down, i),
                ("gate", i): (w_ple_gate, i)}

    cast = {}

    def bf16_weight(name, source):
        if name in cast:
            return cast.pop(name)[None], 0
        stacked, layer = source
        return stacked.astype(_BF16), layer

    def ride(call, wanted):
        names = list(wanted)
        main, *casts = call([wanted[n] for n in names])
        cast.update(zip(names, casts))
        return main

    def q_group(j):
        return dict(params=[(ln_mix_b, j), bf16_weight(("q", j), (w_q, j)), (pairs(g_q), j)],
                    n_norm=d // LANES, out_scale=-LOG2_E * HEAD_DIM ** -0.5, segment_major=False)

    kv_shared = q = None
    for i in range(depth):
        if i < n_a:
            mixer_params = [(ln_mix_a, i), (w_in_a, i), (g_v_a, i), (w_spatial, i), (bias, i)]
            wanted = post_weights(i) if i == 0 else {}
            y = ride(lambda riders: _sgu_call(xf, mixer_params, riders), wanted)
        else:
            if q is None:
                q, = _proj_call(xf, [q_group(i - n_a)])
            y = _attn_call(q, kv_shared, batch, seq)
        weights = {n: bf16_weight(n, src) for n, src in post_weights(i).items()}
        post_params = [weights[("out", i)], (ln_mlp, i), weights[("up", i)], weights[("down", i)],
                       (ln_ple, i), weights[("gate", i)], (w_ple_proj, i)]
        wanted = post_weights(i + 1) if i + 1 < depth else {}
        if i == n_a - 1:
            wanted[("kv", 0)] = (w_kv[None], 0)
        if n_a <= i + 1 < depth:
            wanted[("q", i + 1 - n_a)] = (w_q, i + 1 - n_a)
        xf = ride(lambda riders: _post_call(xf, y, (pf, i), post_params, riders), wanted)
        groups = []
        if i == n_a - 1:
            groups.append(dict(
                params=[(rows(ln_kv[None]), 0), bf16_weight(("kv", 0), (w_kv[None], 0)),
                        (pairs(g_k[None]), 0)],
                n_norm=d // LANES, out_scale=1.0, segment_major=True))
        if n_a <= i + 1 < depth:
            groups.append(q_group(i + 1 - n_a))
        projected = list(_proj_call(xf, groups)) if groups else []
        if i == n_a - 1:
            kv_shared = projected.pop(0)
        q = projected.pop(0) if projected else None
    return xf.reshape(batch, seq, d)
```

```python
import functools

import jax
import jax.numpy as jnp
from jax import lax
from jax.experimental import pallas as pl
from jax.experimental.pallas import tpu as pltpu

EPS = 1e-6
CHUNK = 128
HEAD_DIM = 64
LANES = 128
MXU_WIDTH = 256
BF16_ROWS = 16
TOKEN_TILE = 512
WIDE_TOKEN_TILE = 1024
ATTN_BLOCK = 128
STATIC_SWEEP = 3
FF_CHUNK = 1024
VMEM_LIMIT = 56 * 1024 * 1024
SUBLANES = 8
KEY_SEGMENT = ATTN_BLOCK // SUBLANES
LOG2_E = 1.4426950408889634
WEIGHT_FLOOR = 2.0 ** -126

_BF16 = jnp.bfloat16
_F32 = jnp.float32


def _rms(x, g):
    return x * lax.rsqrt(jnp.mean(x * x, axis=-1, keepdims=True) + EPS) * g


def _gelu_tanh(x):
    c = -2.0 * LOG2_E * (2.0 / jnp.pi) ** 0.5
    return x / (1.0 + jnp.exp2(x * (x * x * (c * 0.044715) + c)))


def _layer_spec(param):
    stacked, layer = param
    rest = stacked.shape[1:]
    return pl.BlockSpec((None,) + rest, lambda *_: (layer,) + (0,) * len(rest),
                        pipeline_mode=pl.Buffered(1))


def _rider_specs(riders, n_steps):
    arrays, in_specs, out_specs, out_shapes = [], [], [], []
    for stacked, layer in riders:
        _, r, c = stacked.shape
        rb = r // n_steps
        assert rb * n_steps == r and rb % BF16_ROWS == 0, (stacked.shape, n_steps)
        arrays.append(stacked)
        in_specs.append(pl.BlockSpec((None, rb, c), lambda i, layer=layer: (layer, i, 0)))
        out_specs.append(pl.BlockSpec((rb, c), lambda i: (i, 0)))
        out_shapes.append(jax.ShapeDtypeStruct((r, c), _BF16))
    return arrays, in_specs, out_specs, out_shapes


def _cast_riders(src_refs, dst_refs):
    for src, dst in zip(src_refs, dst_refs):
        dst[...] = src[...].astype(dst.dtype)


def _params(n_axes):
    return pltpu.CompilerParams(
        dimension_semantics=("arbitrary",) * n_axes, vmem_limit_bytes=VMEM_LIMIT)


def _sgu_kernel(x_ref, ln_ref, win_ref, gv_ref, ws_ref, bias_ref, *rest, n_riders):
    y_ref = rest[n_riders]
    _cast_riders(rest[:n_riders], rest[n_riders + 1:])
    tm, d = x_ref.shape
    n_groups = ws_ref.shape[0]
    n_chunks = tm // CHUNK
    h = _rms(x_ref[...], ln_ref[...]).astype(_BF16)
    v = _gelu_tanh(jnp.dot(h, win_ref[:, d:].astype(_BF16), preferred_element_type=_F32))
    v = _rms(v, gv_ref[...]).astype(_BF16)
    u = _gelu_tanh(jnp.dot(h, win_ref[:, :d].astype(_BF16), preferred_element_type=_F32))
    row = lax.broadcasted_iota(jnp.int32, (CHUNK, CHUNK), 0)
    col = lax.broadcasted_iota(jnp.int32, (CHUNK, CHUNK), 1)
    causal = col <= row
    mix_cols = []
    for g in range(n_groups):
        w = jnp.where(causal, ws_ref[g], 0.0).astype(_BF16)
        vg = jnp.concatenate(
            [v[c * CHUNK:(c + 1) * CHUNK, g * LANES:(g + 1) * LANES] for c in range(n_chunks)],
            axis=1)
        mg = jnp.dot(w, vg, preferred_element_type=_F32)
        mix_cols.append(jnp.concatenate(
            [mg[:, c * LANES:(c + 1) * LANES] for c in range(n_chunks)], axis=0))
    mix = jnp.concatenate(mix_cols, axis=1)
    bias = jnp.concatenate([bias_ref[...]] * n_chunks, axis=0)
    y_ref[...] = (u * (mix + bias)).astype(y_ref.dtype)


def _sgu_call(x, params, riders=()):
    m, d = x.shape
    tm = WIDE_TOKEN_TILE
    tile = pl.BlockSpec((tm, d), lambda i: (i, 0))
    r_arrays, r_in, r_out, r_shapes = _rider_specs(riders, m // tm)
    return pl.pallas_call(
        functools.partial(_sgu_kernel, n_riders=len(riders)),
        grid=(m // tm,),
        in_specs=[tile] + [_layer_spec(a) for a in params] + r_in,
        out_specs=[tile] + r_out,
        out_shape=[jax.ShapeDtypeStruct((m, d), _BF16)] + r_shapes,
        compiler_params=_params(1),
        name="sgu_mixer",
    )(x, *[a for a, _ in params], *r_arrays)


def _post_kernel(x_ref, y_ref, p_ref, wo_ref, lnm_ref, wup_ref, wdn_ref,
                 lnp_ref, wg_ref, wp_ref, *rest, n_riders):
    o_ref = rest[n_riders]
    _cast_riders(rest[:n_riders], rest[n_riders + 1:])
    d_ff = wup_ref.shape[1]
    x = x_ref[...] + jnp.dot(y_ref[...], wo_ref[...], preferred_element_type=_F32)
    h = _rms(x, lnm_ref[...]).astype(_BF16)
    acc = x
    for c in range(d_ff // FF_CHUNK):
        a = jnp.dot(h, wup_ref[:, c * FF_CHUNK:(c + 1) * FF_CHUNK], preferred_element_type=_F32)
        a = jnp.maximum(a, 0.0)
        a = (a * a).astype(_BF16)
        acc = acc + jnp.dot(a, wdn_ref[c * FF_CHUNK:(c + 1) * FF_CHUNK, :],
                            preferred_element_type=_F32)
    x = acc
    hg = _rms(x, lnp_ref[...]).astype(_BF16)
    gate = 1.0 / (1.0 + jnp.exp(-jnp.dot(hg, wg_ref[...], preferred_element_type=_F32)))
    pp = jnp.dot(p_ref[...].astype(_BF16), wp_ref[...], preferred_element_type=_F32)
    o_ref[...] = x + pp * gate


def _post_call(x, y, p_param, params, riders=()):
    m, d = x.shape
    tm = TOKEN_TILE
    tile = pl.BlockSpec((tm, d), lambda i: (i, 0))
    p, p_layer = p_param
    r_arrays, r_in, r_out, r_shapes = _rider_specs(riders, m // tm)
    return pl.pallas_call(
        functools.partial(_post_kernel, n_riders=len(riders)),
        grid=(m // tm,),
        in_specs=[tile, tile, pl.BlockSpec((None, tm, p.shape[2]), lambda i: (p_layer, i, 0))]
        + [_layer_spec(a) for a in params] + r_in,
        out_specs=[tile] + r_out,
        out_shape=[jax.ShapeDtypeStruct((m, d), _F32)] + r_shapes,
        compiler_params=_params(1),
        name="post_mlp_ple",
    )(x, y, p, *[a for a, _ in params], *r_arrays)


def _proj_kernel(x_ref, *refs, groups):
    x = x_ref[...]
    for n, cfg in enumerate(groups):
        _project_heads(x, *refs[3 * n:3 * n + 3], refs[3 * len(groups) + n], **cfg)


def _project_heads(x, ln_ref, w_ref, g_ref, o_ref, *, n_norm, out_scale, segment_major):
    tm = x.shape[0]
    h = _rms(x, ln_ref[...]).astype(_BF16)
    if segment_major:
        r = lax.broadcasted_iota(jnp.int32, (ATTN_BLOCK, ATTN_BLOCK), 0)
        c = lax.broadcasted_iota(jnp.int32, (ATTN_BLOCK, ATTN_BLOCK), 1)
        src = KEY_SEGMENT * (r & (SUBLANES - 1)) + r // SUBLANES
        perm = jnp.where(c == src, 1.0, 0.0).astype(_BF16)
        h = jnp.concatenate(
            [jnp.dot(perm, h[b * ATTN_BLOCK:(b + 1) * ATTN_BLOCK],
                     preferred_element_type=_F32).astype(_BF16)
             for b in range(tm // ATTN_BLOCK)], axis=0)
    first_head = lax.broadcasted_iota(jnp.int32, (tm, LANES), 1) < HEAD_DIM
    gain = g_ref[...] * out_scale
    for c in range(w_ref.shape[1] // MXU_WIDTH):
        y = jnp.dot(h, w_ref[:, c * MXU_WIDTH:(c + 1) * MXU_WIDTH], preferred_element_type=_F32)
        for t in range(MXU_WIDTH // LANES):
            blk = y[:, t * LANES:(t + 1) * LANES]
            col = c * MXU_WIDTH + t * LANES
            if col < n_norm * LANES:
                sq = blk * blk
                s_a = jnp.sum(jnp.where(first_head, sq, 0.0), axis=-1, keepdims=True)
                s_b = jnp.sum(jnp.where(first_head, 0.0, sq), axis=-1, keepdims=True)
                ms = jnp.where(first_head, s_a, s_b) * (1.0 / HEAD_DIM)
                blk = blk * (lax.rsqrt(ms + EPS) * gain)
            o_ref[:, col:col + LANES] = blk.astype(o_ref.dtype)


def _proj_call(x, groups):
    m, d = x.shape
    tm = WIDE_TOKEN_TILE
    tile = lambda width: pl.BlockSpec((tm, width), lambda i: (i, 0))
    params = [a for g in groups for a in g["params"]]
    widths = [g["params"][1][0].shape[2] for g in groups]
    cfgs = tuple({k: g[k] for k in ("n_norm", "out_scale", "segment_major")} for g in groups)
    return pl.pallas_call(
        functools.partial(_proj_kernel, groups=cfgs),
        grid=(m // tm,),
        in_specs=[tile(d)] + [_layer_spec(a) for a in params],
        out_specs=[tile(n) for n in widths],
        out_shape=[jax.ShapeDtypeStruct((m, n), _BF16) for n in widths],
        compiler_params=_params(1),
        name="head_proj",
    )(x, *[a for a, _ in params])


def _attn_kernel(q_ref, k_ref, v_ref, o_ref, qs_ref, passed_ref, acc_ref):
    blk = ATTN_BLOCK
    half = blk // 2
    n_pairs = qs_ref.shape[0]
    n_groups = blk // SUBLANES
    first_head = lax.broadcasted_iota(jnp.int32, (half, LANES), 1) < HEAD_DIM

    def block_weights(z, passed, valid):
        beta = 1.0 / (1.0 + jnp.exp2(z))
        keep = 1.0 - beta
        sub = lax.broadcasted_iota(jnp.int32, (SUBLANES, z.shape[1]), 0)
        if valid is not None:
            keep = jnp.where(valid, keep, 1.0)
        after = [None] * n_groups
        run = None
        for v in range(n_groups - 1, -1, -1):
            after[v] = run
            kv = keep[v * SUBLANES:(v + 1) * SUBLANES]
            run = kv if run is None else run * kv
        later = jnp.where(sub + 1 < SUBLANES, pltpu.roll(run, SUBLANES - 1, 0), 1.0)
        for step in (1, 2, 4):
            shifted = pltpu.roll(later, SUBLANES - step, 0)
            later = later * jnp.where(sub + step < SUBLANES, shifted, 1.0)
        whole = later * run
        base = later if passed is None else later * passed
        parts = []
        for v in range(n_groups):
            scale = base if after[v] is None else after[v] * base
            parts.append(beta[v * SUBLANES:(v + 1) * SUBLANES] * scale)
        w = jnp.concatenate(parts, axis=0)
        if valid is not None:
            w = jnp.where(valid, w, 0.0)
        total = jnp.broadcast_to(whole[0:1], whole.shape)
        return w.astype(_BF16), total if passed is None else passed * total

    def sweep(j_last, n_blocks, first, lanes=None, oldest_early_only=False):
        rows = pl.ds(pl.multiple_of((j_last - (n_blocks - 1)) * blk, blk), n_blocks * blk)
        lo, hi = (0, 2 * blk) if lanes is None else lanes
        valid = None
        if first:
            r = lax.broadcasted_iota(jnp.int32, (blk, 2 * blk), 0)
            key = KEY_SEGMENT * (r & (SUBLANES - 1)) + r // SUBLANES
            lane = lax.broadcasted_iota(jnp.int32, (blk, 2 * blk), 1)
            query = (lane & (half - 1)) + jnp.where(lane >= blk, half, 0)
            valid = key < query
        pairs = range(n_pairs)
        zs = [lax.dot_general(k_ref[rows, p * LANES:(p + 1) * LANES], qs_ref[p, lo:hi, :],
                              (((1,), (1,)), ((), ())), preferred_element_type=_F32)
              for p in pairs]
        weights = []
        for p in pairs:
            before = None if first else passed_ref[p]
            passed = None if first else before[:, lo:hi]
            parts = [None] * n_blocks
            for b in range(n_blocks - 1, -1, -1):
                z = zs[p][b * blk:(b + 1) * blk]
                if oldest_early_only and b == 0:
                    w, early = block_weights(z[:, :blk], passed[:, :blk], None)
                    parts[b] = jnp.concatenate([w, jnp.zeros_like(w)], axis=1)
                    passed = jnp.concatenate([early, passed[:, blk:]], axis=1)
                else:
                    parts[b], passed = block_weights(
                        z, passed, valid if first and b == n_blocks - 1 else None)
            w = jnp.concatenate(parts, axis=0) if n_blocks > 1 else parts[0]
            if lanes is not None:
                zero = jnp.zeros((w.shape[0], 2 * blk - (hi - lo)), w.dtype)
                w = jnp.concatenate([zero, w] if lo else [w, zero], axis=1)
                passed = jnp.concatenate(
                    [before[:, :lo], passed] if lo else [passed, before[:, hi:]], axis=1)
            passed_ref[p] = passed
            weights.append(w)
        for p in pairs:
            acc = lax.dot_general(v_ref[rows, p * LANES:(p + 1) * LANES], weights[p],
                                  (((0,), (0,)), ((), ())), preferred_element_type=_F32)
            acc_ref[p] = acc if first else acc_ref[p] + acc

    def largest_passed(lo=0, hi=2 * blk):
        top = passed_ref[0, :, lo:hi]
        for p in range(1, n_pairs):
            top = jnp.maximum(top, passed_ref[p, :, lo:hi])
        return jnp.max(top)

    def query_block(i, carry):
        rows = pl.ds(pl.multiple_of(i * blk, blk), blk)
        first_rows = lax.broadcasted_iota(jnp.int32, (blk, LANES), 1) < HEAD_DIM
        for p in range(n_pairs):
            q2 = q_ref[rows, p * LANES:(p + 1) * LANES]
            zeros = jnp.zeros_like(q2)
            qa = jnp.where(first_rows, q2, zeros)
            qb = jnp.where(first_rows, zeros, q2)
            qs_ref[p] = jnp.concatenate([qa[:half], qb[:half], qa[half:], qb[half:]], axis=0)

        for n in range(1, STATIC_SWEEP + 1):
            full = n == STATIC_SWEEP

            @pl.when((i >= n - 1) if full else (i == n - 1))
            def _(n=n, full=full):
                sweep(i, n, first=True, oldest_early_only=full)

        top_early, top_late = largest_passed(0, blk), largest_passed(blk, 2 * blk)

        @pl.when(jnp.logical_and(i >= STATIC_SWEEP - 1, top_late >= WEIGHT_FLOOR))
        def _():
            sweep(i - (STATIC_SWEEP - 1), 1, first=False, lanes=(blk, 2 * blk))

        def cond(state):
            j, top = state
            return jnp.logical_and(j >= 0, top >= WEIGHT_FLOOR)

        def body(state):
            j, _ = state
            sweep(j, 1, first=False)
            return j - 1, largest_passed()

        lax.while_loop(cond, body, (i - STATIC_SWEEP, jnp.maximum(top_early, top_late)))
        for p in range(n_pairs):
            acc = acc_ref[p]
            halves = []
            for part in (acc[:, :blk].T, acc[:, blk:].T):
                halves.append(jnp.where(first_head, part[:half], part[half:]))
            o_ref[rows, p * LANES:(p + 1) * LANES] = jnp.concatenate(
                halves, axis=0).astype(o_ref.dtype)
        return carry

    lax.fori_loop(0, q_ref.shape[0] // blk, query_block, 0)


def _attn_call(q, kv, batch, seq):
    m, d = q.shape
    blk = ATTN_BLOCK
    n_pairs = d // LANES
    return pl.pallas_call(
        _attn_kernel,
        grid=(batch,),
        in_specs=[
            pl.BlockSpec((seq, d), lambda b: (b, 0)),
            pl.BlockSpec((seq, d), lambda b: (b, 0)),
            pl.BlockSpec((seq, d), lambda b: (b, 1)),
        ],
        out_specs=pl.BlockSpec((seq, d), lambda b: (b, 0)),
        out_shape=jax.ShapeDtypeStruct((m, d), _BF16),
        scratch_shapes=[
            pltpu.VMEM((n_pairs, 2 * blk, LANES), _BF16),
            pltpu.VMEM((n_pairs, SUBLANES, 2 * blk), _F32),
            pltpu.VMEM((n_pairs, LANES, 2 * blk), _F32),
        ],
        compiler_params=_params(1),
        name="stick_breaking",
    )(q, kv, kv)


def kernel(x, p, ln_mix_a, w_in_a, g_v_a, w_spatial, b_spatial, w_out_a, ln_kv, w_kv, g_k,
           ln_mix_b, w_q, g_q, w_out_b, ln_mlp, w_up, w_down, ln_ple, w_ple_gate, w_ple_proj):
    batch, seq, d = x.shape
    depth = p.shape[0]
    n_a = ln_mix_a.shape[0]
    m = batch * seq
    xf = x.reshape(m, d)
    pf = p.reshape(depth, m, p.shape[-1])
    rows = lambda g: g.reshape(g.shape[0], 1, g.shape[1])
    pairs = lambda g: jnp.concatenate([g, g], axis=-1).reshape(g.shape[0], 1, LANES)
    ln_mix_a, g_v_a, ln_mix_b, ln_mlp, ln_ple = map(rows, (ln_mix_a, g_v_a, ln_mix_b, ln_mlp, ln_ple))
    w_ple_proj = w_ple_proj.astype(_BF16)
    bias = jnp.repeat(jnp.swapaxes(b_spatial, 1, 2), d // b_spatial.shape[1], axis=2)

    def post_weights(i):
        out = (w_out_a, i) if i < n_a else (w_out_b, i - n_a)
        return {("out", i): out, ("up", i): (w_up, i), ("down", i): (w_down, i),
                ("gate", i): (w_ple_gate, i)}

    cast = {}

    def bf16_weight(name, source):
        if name in cast:
            return cast.pop(name)[None], 0
        stacked, layer = source
        return stacked.astype(_BF16), layer

    def ride(call, wanted):
        names = list(wanted)
        main, *casts = call([wanted[n] for n in names])
        cast.update(zip(names, casts))
        return main

    def q_group(j):
        return dict(params=[(ln_mix_b, j), bf16_weight(("q", j), (w_q, j)), (pairs(g_q), j)],
                    n_norm=d // LANES, out_scale=-LOG2_E * HEAD_DIM ** -0.5, segment_major=False)

    kv_shared = q = None
    for i in range(depth):
        if i < n_a:
            mixer_params = [(ln_mix_a, i), (w_in_a, i), (g_v_a, i), (w_spatial, i), (bias, i)]
            wanted = post_weights(i) if i == 0 else {}
            y = ride(lambda riders: _sgu_call(xf, mixer_params, riders), wanted)
        else:
            if q is None:
                q, = _proj_call(xf, [q_group(i - n_a)])
            y = _attn_call(q, kv_shared, batch, seq)
        weights = {n: bf16_weight(n, src) for n, src in post_weights(i).items()}
        post_params = [weights[("out", i)], (ln_mlp, i), weights[("up", i)], weights[("down", i)],
                       (ln_ple, i), weights[("gate", i)], (w_ple_proj, i)]
        wanted = post_weights(i + 1) if i + 1 < depth else {}
        if i == n_a - 1:
            wanted[("kv", 0)] = (w_kv[None], 0)
        if n_a <= i + 1 < depth:
            wanted[("q", i + 1 - n_a)] = (w_q, i + 1 - n_a)
        xf = ride(lambda riders: _post_call(xf, y, (pf, i), post_params, riders), wanted)
        groups = []
        if i == n_a - 1:
            groups.append(dict(
                params=[(rows(ln_kv[None]), 0), bf16_weight(("kv", 0), (w_kv[None], 0)),
                        (pairs(g_k[None]), 0)],
                n_norm=d // LANES, out_scale=1.0, segment_major=True))
        if n_a <= i + 1 < depth:
            groups.append(q_group(i + 1 - n_a))
        projected = list(_proj_call(xf, groups)) if groups else []
        if i == n_a - 1:
            kv_shared = projected.pop(0)
        q = projected.pop(0) if projected else None
    return xf.reshape(batch, seq, d)
```

```python
import functools

import jax
import jax.numpy as jnp
from jax import lax
from jax.experimental import pallas as pl
from jax.experimental.pallas import tpu as pltpu

EPS = 1e-6
CHUNK = 128
HEAD_DIM = 64
LANES = 128
MXU_WIDTH = 256
BF16_ROWS = 16
TOKEN_TILE = 512
WIDE_TOKEN_TILE = 1024
ATTN_BLOCK = 128
STATIC_SWEEP = 3
assert STATIC_SWEEP >= 2
FF_CHUNK = 1024
VMEM_LIMIT = 56 * 1024 * 1024
SUBLANES = 8
KEY_SEGMENT = ATTN_BLOCK // SUBLANES
LOG2_E = 1.4426950408889634
WEIGHT_FLOOR = 2.0 ** -126

_BF16 = jnp.bfloat16
_F32 = jnp.float32


def _rms(x, g):
    return x * lax.rsqrt(jnp.mean(x * x, axis=-1, keepdims=True) + EPS) * g


def _gelu_tanh(x):
    c = -2.0 * LOG2_E * (2.0 / jnp.pi) ** 0.5
    return x / (1.0 + jnp.exp2(x * (x * x * (c * 0.044715) + c)))


def _layer_spec(param):
    stacked, layer = param
    rest = stacked.shape[1:]
    return pl.BlockSpec((None,) + rest, lambda *_: (layer,) + (0,) * len(rest),
                        pipeline_mode=pl.Buffered(1))


def _rider_specs(riders, n_steps):
    arrays, in_specs, out_specs, out_shapes = [], [], [], []
    for stacked, layer in riders:
        _, r, c = stacked.shape
        rb = r // n_steps
        assert rb * n_steps == r and rb % BF16_ROWS == 0, (stacked.shape, n_steps)
        arrays.append(stacked)
        in_specs.append(pl.BlockSpec((None, rb, c), lambda i, layer=layer: (layer, i, 0)))
        out_specs.append(pl.BlockSpec((rb, c), lambda i: (i, 0)))
        out_shapes.append(jax.ShapeDtypeStruct((r, c), _BF16))
    return arrays, in_specs, out_specs, out_shapes


def _cast_riders(src_refs, dst_refs):
    for src, dst in zip(src_refs, dst_refs):
        dst[...] = src[...].astype(dst.dtype)


def _params(n_axes):
    return pltpu.CompilerParams(
        dimension_semantics=("arbitrary",) * n_axes, vmem_limit_bytes=VMEM_LIMIT)


def _sgu_kernel(x_ref, ln_ref, win_ref, gv_ref, ws_ref, bias_ref, *rest, n_riders):
    y_ref = rest[n_riders]
    _cast_riders(rest[:n_riders], rest[n_riders + 1:])
    tm, d = x_ref.shape
    n_groups = ws_ref.shape[0]
    n_chunks = tm // CHUNK
    h = _rms(x_ref[...], ln_ref[...]).astype(_BF16)
    v = _gelu_tanh(jnp.dot(h, win_ref[:, d:].astype(_BF16), preferred_element_type=_F32))
    v = _rms(v, gv_ref[...]).astype(_BF16)
    u = _gelu_tanh(jnp.dot(h, win_ref[:, :d].astype(_BF16), preferred_element_type=_F32))
    row = lax.broadcasted_iota(jnp.int32, (CHUNK, CHUNK), 0)
    col = lax.broadcasted_iota(jnp.int32, (CHUNK, CHUNK), 1)
    causal = col <= row
    mix_cols = []
    for g in range(n_groups):
        w = jnp.where(causal, ws_ref[g], 0.0).astype(_BF16)
        vg = jnp.concatenate(
            [v[c * CHUNK:(c + 1) * CHUNK, g * LANES:(g + 1) * LANES] for c in range(n_chunks)],
            axis=1)
        mg = jnp.dot(w, vg, preferred_element_type=_F32)
        mix_cols.append(jnp.concatenate(
            [mg[:, c * LANES:(c + 1) * LANES] for c in range(n_chunks)], axis=0))
    mix = jnp.concatenate(mix_cols, axis=1)
    bias = jnp.concatenate([bias_ref[...]] * n_chunks, axis=0)
    y_ref[...] = (u * (mix + bias)).astype(y_ref.dtype)


def _sgu_call(x, params, riders=()):
    m, d = x.shape
    tm = WIDE_TOKEN_TILE
    tile = pl.BlockSpec((tm, d), lambda i: (i, 0))
    r_arrays, r_in, r_out, r_shapes = _rider_specs(riders, m // tm)
    return pl.pallas_call(
        functools.partial(_sgu_kernel, n_riders=len(riders)),
        grid=(m // tm,),
        in_specs=[tile] + [_layer_spec(a) for a in params] + r_in,
        out_specs=[tile] + r_out,
        out_shape=[jax.ShapeDtypeStruct((m, d), _BF16)] + r_shapes,
        compiler_params=_params(1),
        name="sgu_mixer",
    )(x, *[a for a, _ in params], *r_arrays)


def _post_kernel(x_ref, y_ref, p_ref, wo_ref, lnm_ref, wup_ref, wdn_ref,
                 lnp_ref, wg_ref, wp_ref, *rest, n_riders):
    o_ref = rest[n_riders]
    _cast_riders(rest[:n_riders], rest[n_riders + 1:])
    d_ff = wup_ref.shape[1]
    x = x_ref[...] + jnp.dot(y_ref[...], wo_ref[...], preferred_element_type=_F32)
    h = _rms(x, lnm_ref[...]).astype(_BF16)
    acc = x
    for c in range(d_ff // FF_CHUNK):
        a = jnp.dot(h, wup_ref[:, c * FF_CHUNK:(c + 1) * FF_CHUNK], preferred_element_type=_F32)
        a = jnp.maximum(a, 0.0)
        a = (a * a).astype(_BF16)
        acc = acc + jnp.dot(a, wdn_ref[c * FF_CHUNK:(c + 1) * FF_CHUNK, :],
                            preferred_element_type=_F32)
    x = acc
    hg = _rms(x, lnp_ref[...]).astype(_BF16)
    gate = 1.0 / (1.0 + jnp.exp(-jnp.dot(hg, wg_ref[...], preferred_element_type=_F32)))
    pp = jnp.dot(p_ref[...].astype(_BF16), wp_ref[...], preferred_element_type=_F32)
    o_ref[...] = x + pp * gate


def _post_call(x, y, p_param, params, riders=()):
    m, d = x.shape
    tm = TOKEN_TILE
    tile = pl.BlockSpec((tm, d), lambda i: (i, 0))
    p, p_layer = p_param
    r_arrays, r_in, r_out, r_shapes = _rider_specs(riders, m // tm)
    return pl.pallas_call(
        functools.partial(_post_kernel, n_riders=len(riders)),
        grid=(m // tm,),
        in_specs=[tile, tile, pl.BlockSpec((None, tm, p.shape[2]), lambda i: (p_layer, i, 0))]
        + [_layer_spec(a) for a in params] + r_in,
        out_specs=[tile] + r_out,
        out_shape=[jax.ShapeDtypeStruct((m, d), _F32)] + r_shapes,
        compiler_params=_params(1),
        name="post_mlp_ple",
    )(x, y, p, *[a for a, _ in params], *r_arrays)


def _proj_kernel(x_ref, *refs, groups):
    x = x_ref[...]
    for n, cfg in enumerate(groups):
        _project_heads(x, *refs[3 * n:3 * n + 3], refs[3 * len(groups) + n], **cfg)


def _project_heads(x, ln_ref, w_ref, g_ref, o_ref, *, n_norm, out_scale, segment_major):
    tm = x.shape[0]
    h = _rms(x, ln_ref[...]).astype(_BF16)
    if segment_major:
        r = lax.broadcasted_iota(jnp.int32, (ATTN_BLOCK, ATTN_BLOCK), 0)
        c = lax.broadcasted_iota(jnp.int32, (ATTN_BLOCK, ATTN_BLOCK), 1)
        src = KEY_SEGMENT * (r & (SUBLANES - 1)) + r // SUBLANES
        perm = jnp.where(c == src, 1.0, 0.0).astype(_BF16)
        h = jnp.concatenate(
            [jnp.dot(perm, h[b * ATTN_BLOCK:(b + 1) * ATTN_BLOCK],
                     preferred_element_type=_F32).astype(_BF16)
             for b in range(tm // ATTN_BLOCK)], axis=0)
    first_head = lax.broadcasted_iota(jnp.int32, (tm, LANES), 1) < HEAD_DIM
    gain = g_ref[...] * out_scale
    for c in range(w_ref.shape[1] // MXU_WIDTH):
        y = jnp.dot(h, w_ref[:, c * MXU_WIDTH:(c + 1) * MXU_WIDTH], preferred_element_type=_F32)
        for t in range(MXU_WIDTH // LANES):
            blk = y[:, t * LANES:(t + 1) * LANES]
            col = c * MXU_WIDTH + t * LANES
            if col < n_norm * LANES:
                sq = blk * blk
                s_a = jnp.sum(jnp.where(first_head, sq, 0.0), axis=-1, keepdims=True)
                s_b = jnp.sum(jnp.where(first_head, 0.0, sq), axis=-1, keepdims=True)
                ms = jnp.where(first_head, s_a, s_b) * (1.0 / HEAD_DIM)
                blk = blk * (lax.rsqrt(ms + EPS) * gain)
            o_ref[:, col:col + LANES] = blk.astype(o_ref.dtype)


def _proj_call(x, groups):
    m, d = x.shape
    tm = WIDE_TOKEN_TILE
    tile = lambda width: pl.BlockSpec((tm, width), lambda i: (i, 0))
    params = [a for g in groups for a in g["params"]]
    widths = [g["params"][1][0].shape[2] for g in groups]
    cfgs = tuple({k: g[k] for k in ("n_norm", "out_scale", "segment_major")} for g in groups)
    return pl.pallas_call(
        functools.partial(_proj_kernel, groups=cfgs),
        grid=(m // tm,),
        in_specs=[tile(d)] + [_layer_spec(a) for a in params],
        out_specs=[tile(n) for n in widths],
        out_shape=[jax.ShapeDtypeStruct((m, n), _BF16) for n in widths],
        compiler_params=_params(1),
        name="head_proj",
    )(x, *[a for a, _ in params])


def _attn_kernel(q_ref, k_ref, v_ref, o_ref, qs_ref, passed_ref, acc_ref, bound_ref):
    blk = ATTN_BLOCK
    n_pairs = qs_ref.shape[0]
    n_groups = blk // SUBLANES
    lane = lax.broadcasted_iota(jnp.int32, (blk, LANES), 1)
    first_head = lane < HEAD_DIM
    sub = lax.broadcasted_iota(jnp.int32, (SUBLANES, 2 * blk), 0)

    def block_weights(z, passed, valid):
        beta = 1.0 / (1.0 + jnp.exp2(z))
        keep = 1.0 - beta
        if valid is not None:
            keep = jnp.where(valid, keep, 1.0)
        after = [None] * n_groups
        run = None
        for v in range(n_groups - 1, -1, -1):
            after[v] = run
            kv = keep[v * SUBLANES:(v + 1) * SUBLANES]
            run = kv if run is None else run * kv
        later = jnp.where(sub + 1 < SUBLANES, pltpu.roll(run, SUBLANES - 1, 0), 1.0)
        for step in (1, 2, 4):
            shifted = pltpu.roll(later, SUBLANES - step, 0)
            later = later * jnp.where(sub + step < SUBLANES, shifted, 1.0)
        whole = later * run
        base = later if passed is None else later * passed
        parts = []
        for v in range(n_groups):
            scale = base if after[v] is None else after[v] * base
            parts.append(beta[v * SUBLANES:(v + 1) * SUBLANES] * scale)
        w = jnp.concatenate(parts, axis=0)
        if valid is not None:
            w = jnp.where(valid, w, 0.0)
        total = jnp.broadcast_to(whole[0:1], whole.shape)
        return w.astype(_BF16), total if passed is None else passed * total

    def sweep(j_last, n_blocks, first, slot):
        rows = pl.ds(pl.multiple_of((j_last - (n_blocks - 1)) * blk, blk), n_blocks * blk)
        valid = None
        if first:
            r = lax.broadcasted_iota(jnp.int32, (blk, 2 * blk), 0)
            key = KEY_SEGMENT * (r & (SUBLANES - 1)) + r // SUBLANES
            query = lax.broadcasted_iota(jnp.int32, (blk, 2 * blk), 1) & (blk - 1)
            valid = key < query
        pairs = range(n_pairs)
        zs = [lax.dot_general(k_ref[rows, p * LANES:(p + 1) * LANES], qs_ref[p],
                              (((1,), (1,)), ((), ())), preferred_element_type=_F32)
              for p in pairs]
        weights = []
        top = None
        for p in pairs:
            passed = None if first else passed_ref[p]
            parts = [None] * n_blocks
            for b in range(n_blocks - 1, -1, -1):
                parts[b], passed = block_weights(
                    zs[p][b * blk:(b + 1) * blk], passed,
                    valid if first and b == n_blocks - 1 else None)
            passed_ref[p] = passed
            top = passed if top is None else jnp.maximum(top, passed)
            weights.append(jnp.concatenate(parts, axis=0) if n_blocks > 1 else parts[0])
        bound_ref[0] = jnp.max(top)
        for p in pairs:
            acc = lax.dot_general(v_ref[rows, p * LANES:(p + 1) * LANES], weights[p],
                                  (((0,), (0,)), ((), ())), preferred_element_type=_F32)
            acc_ref[slot, p] = acc if first else acc_ref[slot, p] + acc

    def write_out(i, slot):
        rows = pl.ds(pl.multiple_of(i * blk, blk), blk)
        for p in range(n_pairs):
            acc = acc_ref[slot, p]
            out_t = jnp.concatenate([acc[:HEAD_DIM, :blk], acc[HEAD_DIM:, blk:]], axis=0)
            o_ref[rows, p * LANES:(p + 1) * LANES] = out_t.T.astype(o_ref.dtype)

    def query_block(i, carry):
        rows = pl.ds(pl.multiple_of(i * blk, blk), blk)
        slot = i & 1
        for p in range(n_pairs):
            q2 = q_ref[rows, p * LANES:(p + 1) * LANES]
            zeros = jnp.zeros_like(q2)
            qs_ref[p] = jnp.concatenate(
                [jnp.where(first_head, q2, zeros), jnp.where(first_head, zeros, q2)], axis=0)

        for n in range(1, STATIC_SWEEP + 1):
            @pl.when((i == n - 1) if n < STATIC_SWEEP else (i >= n - 1))
            def _(n=n):
                if n > 1:
                    write_out(i - 1, 1 - slot)
                sweep(i, n, first=True, slot=slot)

        def cond(state):
            j, top = state
            return jnp.logical_and(j >= 0, top >= WEIGHT_FLOOR)

        def body(state):
            j, _ = state
            sweep(j, 1, first=False, slot=slot)
            return j - 1, bound_ref[0]

        lax.while_loop(cond, body, (i - STATIC_SWEEP, bound_ref[0]))
        return carry

    n_q = q_ref.shape[0] // blk
    lax.fori_loop(0, n_q, query_block, 0)
    write_out(n_q - 1, (n_q - 1) & 1)


def _attn_call(q, kv, batch, seq):
    m, d = q.shape
    blk = ATTN_BLOCK
    n_pairs = d // LANES
    return pl.pallas_call(
        _attn_kernel,
        grid=(batch,),
        in_specs=[
            pl.BlockSpec((seq, d), lambda b: (b, 0)),
            pl.BlockSpec((seq, d), lambda b: (b, 0)),
            pl.BlockSpec((seq, d), lambda b: (b, 1)),
        ],
        out_specs=pl.BlockSpec((seq, d), lambda b: (b, 0)),
        out_shape=jax.ShapeDtypeStruct((m, d), _BF16),
        scratch_shapes=[
            pltpu.VMEM((n_pairs, 2 * blk, LANES), _BF16),
            pltpu.VMEM((n_pairs, SUBLANES, 2 * blk), _F32),
            pltpu.VMEM((2, n_pairs, LANES, 2 * blk), _F32),
            pltpu.SMEM((1,), _F32),
        ],
        compiler_params=_params(1),
        name="stick_breaking",
    )(q, kv, kv)


def kernel(x, p, ln_mix_a, w_in_a, g_v_a, w_spatial, b_spatial, w_out_a, ln_kv, w_kv, g_k,
           ln_mix_b, w_q, g_q, w_out_b, ln_mlp, w_up, w_down, ln_ple, w_ple_gate, w_ple_proj):
    batch, seq, d = x.shape
    depth = p.shape[0]
    n_a = ln_mix_a.shape[0]
    m = batch * seq
    xf = x.reshape(m, d)
    pf = p.reshape(depth, m, p.shape[-1])
    rows = lambda g: g.reshape(g.shape[0], 1, g.shape[1])
    pairs = lambda g: jnp.concatenate([g, g], axis=-1).reshape(g.shape[0], 1, LANES)
    ln_mix_a, g_v_a, ln_mix_b, ln_mlp, ln_ple = map(rows, (ln_mix_a, g_v_a, ln_mix_b, ln_mlp, ln_ple))
    w_ple_proj = w_ple_proj.astype(_BF16)
    bias = jnp.repeat(jnp.swapaxes(b_spatial, 1, 2), d // b_spatial.shape[1], axis=2)

    def post_weights(i):
        out = (w_out_a, i) if i < n_a else (w_out_b, i - n_a)
        return {("out", i): out, ("up", i): (w_up, i), ("down", i): (w_down, i),
                ("gate", i): (w_ple_gate, i)}

    cast = {}

    def bf16_weight(name, source):
        if name in cast:
            return cast.pop(name)[None], 0
        stacked, layer = source
        return stacked.astype(_BF16), layer

    def ride(call, wanted):
        names = list(wanted)
        main, *casts = call([wanted[n] for n in names])
        cast.update(zip(names, casts))
        return main

    def q_group(j):
        return dict(params=[(ln_mix_b, j), bf16_weight(("q", j), (w_q, j)), (pairs(g_q), j)],
                    n_norm=d // LANES, out_scale=-LOG2_E * HEAD_DIM ** -0.5, segment_major=False)

    kv_shared = q = None
    for i in range(depth):
        if i < n_a:
            mixer_params = [(ln_mix_a, i), (w_in_a, i), (g_v_a, i), (w_spatial, i), (bias, i)]
            wanted = post_weights(i) if i == 0 else {}
            y = ride(lambda riders: _sgu_call(xf, mixer_params, riders), wanted)
        else:
            if q is None:
                q, = _proj_call(xf, [q_group(i - n_a)])
            y = _attn_call(q, kv_shared, batch, seq)
        weights = {n: bf16_weight(n, src) for n, src in post_weights(i).items()}
        post_params = [weights[("out", i)], (ln_mlp, i), weights[("up", i)], weights[("down", i)],
                       (ln_ple, i), weights[("gate", i)], (w_ple_proj, i)]
        wanted = post_weights(i + 1) if i + 1 < depth else {}
        if i == n_a - 1:
            wanted[("kv", 0)] = (w_kv[None], 0)
        if n_a <= i + 1 < depth:
            wanted[("q", i + 1 - n_a)] = (w_q, i + 1 - n_a)
        xf = ride(lambda riders: _post_call(xf, y, (pf, i), post_params, riders), wanted)
        groups = []
        if i == n_a - 1:
            groups.append(dict(
                params=[(rows(ln_kv[None]), 0), bf16_weight(("kv", 0), (w_kv[None], 0)),
                        (pairs(g_k[None]), 0)],
                n_norm=d // LANES, out_scale=1.0, segment_major=True))
        if n_a <= i + 1 < depth:
            groups.append(q_group(i + 1 - n_a))
        projected = list(_proj_call(xf, groups)) if groups else []
        if i == n_a - 1:
            kv_shared = projected.pop(0)
        q = projected.pop(0) if projected else None
    return xf.reshape(batch, seq, d)
```

```python
import functools

import jax
import jax.numpy as jnp
from jax import lax
from jax.experimental import pallas as pl
from jax.experimental.pallas import tpu as pltpu

EPS = 1e-6
CHUNK = 128
HEAD_DIM = 64
LANES = 128
MXU_WIDTH = 256
BF16_ROWS = 16
TOKEN_TILE = 512
WIDE_TOKEN_TILE = 1024
ATTN_BLOCK = 128
STATIC_SWEEP = 3
assert STATIC_SWEEP >= 2
FF_CHUNK = 1024
VMEM_LIMIT = 56 * 1024 * 1024
SUBLANES = 8
KEY_SEGMENT = ATTN_BLOCK // SUBLANES
LOG2_E = 1.4426950408889634
WEIGHT_FLOOR = 2.0 ** -126

_BF16 = jnp.bfloat16
_F32 = jnp.float32


def _rms(x, g):
    return x * lax.rsqrt(jnp.mean(x * x, axis=-1, keepdims=True) + EPS) * g


def _gelu_tanh(x):
    c = -2.0 * LOG2_E * (2.0 / jnp.pi) ** 0.5
    return x / (1.0 + jnp.exp2(x * (x * x * (c * 0.044715) + c)))


def _layer_spec(param):
    stacked, layer = param
    rest = stacked.shape[1:]
    return pl.BlockSpec((None,) + rest, lambda *_: (layer,) + (0,) * len(rest),
                        pipeline_mode=pl.Buffered(1))


def _rider_specs(riders, n_steps):
    arrays, in_specs, out_specs, out_shapes = [], [], [], []
    for stacked, layer in riders:
        _, r, c = stacked.shape
        rb = r // n_steps
        assert rb * n_steps == r and rb % BF16_ROWS == 0, (stacked.shape, n_steps)
        arrays.append(stacked)
        in_specs.append(pl.BlockSpec((None, rb, c), lambda i, layer=layer: (layer, i, 0)))
        out_specs.append(pl.BlockSpec((rb, c), lambda i: (i, 0)))
        out_shapes.append(jax.ShapeDtypeStruct((r, c), _BF16))
    return arrays, in_specs, out_specs, out_shapes


def _cast_riders(src_refs, dst_refs):
    for src, dst in zip(src_refs, dst_refs):
        dst[...] = src[...].astype(dst.dtype)


def _params(n_axes):
    return pltpu.CompilerParams(
        dimension_semantics=("arbitrary",) * n_axes, vmem_limit_bytes=VMEM_LIMIT)


def _sgu_kernel(x_ref, ln_ref, win_ref, gv_ref, ws_ref, bias_ref, *rest, n_riders):
    y_ref = rest[n_riders]
    _cast_riders(rest[:n_riders], rest[n_riders + 1:])
    tm, d = x_ref.shape
    n_groups = ws_ref.shape[0]
    n_chunks = tm // CHUNK
    h = _rms(x_ref[...], ln_ref[...]).astype(_BF16)
    v = _gelu_tanh(jnp.dot(h, win_ref[:, d:].astype(_BF16), preferred_element_type=_F32))
    v = _rms(v, gv_ref[...]).astype(_BF16)
    u_cols = d // MXU_WIDTH
    u_tiles = [_gelu_tanh(jnp.dot(h, win_ref[:, c * MXU_WIDTH:(c + 1) * MXU_WIDTH].astype(_BF16),
                                  preferred_element_type=_F32)) for c in range(u_cols - 1)]
    row = lax.broadcasted_iota(jnp.int32, (CHUNK, CHUNK), 0)
    col = lax.broadcasted_iota(jnp.int32, (CHUNK, CHUNK), 1)
    causal = col <= row
    mix_cols = []
    for g in range(n_groups):
        w = jnp.where(causal, ws_ref[g], 0.0).astype(_BF16)
        vg = jnp.concatenate(
            [v[c * CHUNK:(c + 1) * CHUNK, g * LANES:(g + 1) * LANES] for c in range(n_chunks)],
            axis=1)
        mg = jnp.dot(w, vg, preferred_element_type=_F32)
        mix_cols.append(jnp.concatenate(
            [mg[:, c * LANES:(c + 1) * LANES] for c in range(n_chunks)], axis=0))
    mix = jnp.concatenate(mix_cols, axis=1)
    bias = jnp.concatenate([bias_ref[...]] * n_chunks, axis=0)
    u_tiles.append(_gelu_tanh(jnp.dot(h, win_ref[:, d - MXU_WIDTH:d].astype(_BF16),
                                      preferred_element_type=_F32)))
    for c, u in enumerate(u_tiles):
        cols = slice(c * MXU_WIDTH, (c + 1) * MXU_WIDTH)
        y_ref[:, cols] = (u * (mix[:, cols] + bias[:, cols])).astype(y_ref.dtype)


def _sgu_call(x, params, riders=()):
    m, d = x.shape
    tm = WIDE_TOKEN_TILE
    tile = pl.BlockSpec((tm, d), lambda i: (i, 0))
    r_arrays, r_in, r_out, r_shapes = _rider_specs(riders, m // tm)
    return pl.pallas_call(
        functools.partial(_sgu_kernel, n_riders=len(riders)),
        grid=(m // tm,),
        in_specs=[tile] + [_layer_spec(a) for a in params] + r_in,
        out_specs=[tile] + r_out,
        out_shape=[jax.ShapeDtypeStruct((m, d), _BF16)] + r_shapes,
        compiler_params=_params(1),
        name="sgu_mixer",
    )(x, *[a for a, _ in params], *r_arrays)


def _post_kernel(x_ref, y_ref, p_ref, wo_ref, lnm_ref, wup_ref, wdn_ref,
                 lnp_ref, wg_ref, wp_ref, *rest, n_riders):
    o_ref = rest[n_riders]
    _cast_riders(rest[:n_riders], rest[n_riders + 1:])
    d_ff = wup_ref.shape[1]
    x = x_ref[...] + jnp.dot(y_ref[...], wo_ref[...], preferred_element_type=_F32)
    h = _rms(x, lnm_ref[...]).astype(_BF16)
    acc = x
    for c in range(d_ff // FF_CHUNK):
        a = jnp.dot(h, wup_ref[:, c * FF_CHUNK:(c + 1) * FF_CHUNK], preferred_element_type=_F32)
        a = jnp.maximum(a, 0.0)
        a = (a * a).astype(_BF16)
        acc = acc + jnp.dot(a, wdn_ref[c * FF_CHUNK:(c + 1) * FF_CHUNK, :],
                            preferred_element_type=_F32)
    x = acc
    hg = _rms(x, lnp_ref[...]).astype(_BF16)
    gate = 1.0 / (1.0 + jnp.exp(-jnp.dot(hg, wg_ref[...], preferred_element_type=_F32)))
    pp = jnp.dot(p_ref[...].astype(_BF16), wp_ref[...], preferred_element_type=_F32)
    o_ref[...] = x + pp * gate


def _post_call(x, y, p_param, params, riders=()):
    m, d = x.shape
    tm = TOKEN_TILE
    tile = pl.BlockSpec((tm, d), lambda i: (i, 0))
    p, p_layer = p_param
    r_arrays, r_in, r_out, r_shapes = _rider_specs(riders, m // tm)
    return pl.pallas_call(
        functools.partial(_post_kernel, n_riders=len(riders)),
        grid=(m // tm,),
        in_specs=[tile, tile, pl.BlockSpec((None, tm, p.shape[2]), lambda i: (p_layer, i, 0))]
        + [_layer_spec(a) for a in params] + r_in,
        out_specs=[tile] + r_out,
        out_shape=[jax.ShapeDtypeStruct((m, d), _F32)] + r_shapes,
        compiler_params=_params(1),
        name="post_mlp_ple",
    )(x, y, p, *[a for a, _ in params], *r_arrays)


def _proj_kernel(x_ref, *refs, groups):
    x = x_ref[...]
    for n, cfg in enumerate(groups):
        _project_heads(x, *refs[3 * n:3 * n + 3], refs[3 * len(groups) + n], **cfg)


def _project_heads(x, ln_ref, w_ref, g_ref, o_ref, *, n_norm, out_scale, segment_major):
    tm = x.shape[0]
    h = _rms(x, ln_ref[...]).astype(_BF16)
    if segment_major:
        r = lax.broadcasted_iota(jnp.int32, (ATTN_BLOCK, ATTN_BLOCK), 0)
        c = lax.broadcasted_iota(jnp.int32, (ATTN_BLOCK, ATTN_BLOCK), 1)
        src = KEY_SEGMENT * (r & (SUBLANES - 1)) + r // SUBLANES
        perm = jnp.where(c == src, 1.0, 0.0).astype(_BF16)
        h = jnp.concatenate(
            [jnp.dot(perm, h[b * ATTN_BLOCK:(b + 1) * ATTN_BLOCK],
                     preferred_element_type=_F32).astype(_BF16)
             for b in range(tm // ATTN_BLOCK)], axis=0)
    first_head = lax.broadcasted_iota(jnp.int32, (tm, LANES), 1) < HEAD_DIM
    gain = g_ref[...] * out_scale
    for c in range(w_ref.shape[1] // MXU_WIDTH):
        y = jnp.dot(h, w_ref[:, c * MXU_WIDTH:(c + 1) * MXU_WIDTH], preferred_element_type=_F32)
        for t in range(MXU_WIDTH // LANES):
            blk = y[:, t * LANES:(t + 1) * LANES]
            col = c * MXU_WIDTH + t * LANES
            if col < n_norm * LANES:
                sq = blk * blk
                s_a = jnp.sum(jnp.where(first_head, sq, 0.0), axis=-1, keepdims=True)
                s_b = jnp.sum(jnp.where(first_head, 0.0, sq), axis=-1, keepdims=True)
                ms = jnp.where(first_head, s_a, s_b) * (1.0 / HEAD_DIM)
                blk = blk * (lax.rsqrt(ms + EPS) * gain)
            o_ref[:, col:col + LANES] = blk.astype(o_ref.dtype)


def _proj_call(x, groups):
    m, d = x.shape
    tm = WIDE_TOKEN_TILE
    tile = lambda width: pl.BlockSpec((tm, width), lambda i: (i, 0))
    params = [a for g in groups for a in g["params"]]
    widths = [g["params"][1][0].shape[2] for g in groups]
    cfgs = tuple({k: g[k] for k in ("n_norm", "out_scale", "segment_major")} for g in groups)
    return pl.pallas_call(
        functools.partial(_proj_kernel, groups=cfgs),
        grid=(m // tm,),
        in_specs=[tile(d)] + [_layer_spec(a) for a in params],
        out_specs=[tile(n) for n in widths],
        out_shape=[jax.ShapeDtypeStruct((m, n), _BF16) for n in widths],
        compiler_params=_params(1),
        name="head_proj",
    )(x, *[a for a, _ in params])


def _attn_kernel(q_ref, k_ref, v_ref, o_ref, qs_ref, passed_ref, acc_ref, bound_ref):
    blk = ATTN_BLOCK
    n_pairs = qs_ref.shape[0]
    n_groups = blk // SUBLANES
    lane = lax.broadcasted_iota(jnp.int32, (blk, LANES), 1)
    first_head = lane < HEAD_DIM
    sub = lax.broadcasted_iota(jnp.int32, (SUBLANES, 2 * blk), 0)

    def block_weights(z, passed, valid):
        beta = 1.0 / (1.0 + jnp.exp2(z))
        keep = 1.0 - beta
        if valid is not None:
            keep = jnp.where(valid, keep, 1.0)
        after = [None] * n_groups
        run = None
        for v in range(n_groups - 1, -1, -1):
            after[v] = run
            kv = keep[v * SUBLANES:(v + 1) * SUBLANES]
            run = kv if run is None else run * kv
        later = jnp.where(sub + 1 < SUBLANES, pltpu.roll(run, SUBLANES - 1, 0), 1.0)
        for step in (1, 2, 4):
            shifted = pltpu.roll(later, SUBLANES - step, 0)
            later = later * jnp.where(sub + step < SUBLANES, shifted, 1.0)
        whole = later * run
        base = later if passed is None else later * passed
        parts = []
        for v in range(n_groups):
            scale = base if after[v] is None else after[v] * base
            parts.append(beta[v * SUBLANES:(v + 1) * SUBLANES] * scale)
        w = jnp.concatenate(parts, axis=0)
        if valid is not None:
            w = jnp.where(valid, w, 0.0)
        total = jnp.broadcast_to(whole[0:1], whole.shape)
        return w.astype(_BF16), total if passed is None else passed * total

    def sweep(j_last, n_blocks, first, slot):
        rows = pl.ds(pl.multiple_of((j_last - (n_blocks - 1)) * blk, blk), n_blocks * blk)
        valid = None
        if first:
            r = lax.broadcasted_iota(jnp.int32, (blk, 2 * blk), 0)
            key = KEY_SEGMENT * (r & (SUBLANES - 1)) + r // SUBLANES
            query = lax.broadcasted_iota(jnp.int32, (blk, 2 * blk), 1) & (blk - 1)
            valid = key < query
        pairs = range(n_pairs)
        zs = [lax.dot_general(k_ref[rows, p * LANES:(p + 1) * LANES], qs_ref[p],
                              (((1,), (1,)), ((), ())), preferred_element_type=_F32)
              for p in pairs]
        weights = []
        top = None
        for p in pairs:
            passed = None if first else passed_ref[p]
            parts = [None] * n_blocks
            for b in range(n_blocks - 1, -1, -1):
                parts[b], passed = block_weights(
                    zs[p][b * blk:(b + 1) * blk], passed,
                    valid if first and b == n_blocks - 1 else None)
            passed_ref[p] = passed
            top = passed if top is None else jnp.maximum(top, passed)
            weights.append(jnp.concatenate(parts, axis=0) if n_blocks > 1 else parts[0])
        bound_ref[0] = jnp.max(top)
        for p in pairs:
            acc = lax.dot_general(v_ref[rows, p * LANES:(p + 1) * LANES], weights[p],
                                  (((0,), (0,)), ((), ())), preferred_element_type=_F32)
            acc_ref[slot, p] = acc if first else acc_ref[slot, p] + acc

    def write_out(i, slot):
        rows = pl.ds(pl.multiple_of(i * blk, blk), blk)
        for p in range(n_pairs):
            acc = acc_ref[slot, p]
            out_t = jnp.concatenate([acc[:HEAD_DIM, :blk], acc[HEAD_DIM:, blk:]], axis=0)
            o_ref[rows, p * LANES:(p + 1) * LANES] = out_t.T.astype(o_ref.dtype)

    def query_block(i, carry):
        rows = pl.ds(pl.multiple_of(i * blk, blk), blk)
        slot = i & 1
        for p in range(n_pairs):
            q2 = q_ref[rows, p * LANES:(p + 1) * LANES]
            zeros = jnp.zeros_like(q2)
            qs_ref[p] = jnp.concatenate(
                [jnp.where(first_head, q2, zeros), jnp.where(first_head, zeros, q2)], axis=0)

        for n in range(1, STATIC_SWEEP + 1):
            @pl.when((i == n - 1) if n < STATIC_SWEEP else (i >= n - 1))
            def _(n=n):
                if n > 1:
                    write_out(i - 1, 1 - slot)
                sweep(i, n, first=True, slot=slot)

        def cond(state):
            j, top = state
            return jnp.logical_and(j >= 0, top >= WEIGHT_FLOOR)

        def body(state):
            j, _ = state
            sweep(j, 1, first=False, slot=slot)
            return j - 1, bound_ref[0]

        lax.while_loop(cond, body, (i - STATIC_SWEEP, bound_ref[0]))
        return carry

    n_q = q_ref.shape[0] // blk
    lax.fori_loop(0, n_q, query_block, 0)
    write_out(n_q - 1, (n_q - 1) & 1)


def _attn_call(q, kv, batch, seq):
    m, d = q.shape
    blk = ATTN_BLOCK
    n_pairs = d // LANES
    return pl.pallas_call(
        _attn_kernel,
        grid=(batch,),
        in_specs=[
            pl.BlockSpec((seq, d), lambda b: (b, 0)),
            pl.BlockSpec((seq, d), lambda b: (b, 0)),
            pl.BlockSpec((seq, d), lambda b: (b, 1)),
        ],
        out_specs=pl.BlockSpec((seq, d), lambda b: (b, 0)),
        out_shape=jax.ShapeDtypeStruct((m, d), _BF16),
        scratch_shapes=[
            pltpu.VMEM((n_pairs, 2 * blk, LANES), _BF16),
            pltpu.VMEM((n_pairs, SUBLANES, 2 * blk), _F32),
            pltpu.VMEM((2, n_pairs, LANES, 2 * blk), _F32),
            pltpu.SMEM((1,), _F32),
        ],
        compiler_params=_params(1),
        name="stick_breaking",
    )(q, kv, kv)


def kernel(x, p, ln_mix_a, w_in_a, g_v_a, w_spatial, b_spatial, w_out_a, ln_kv, w_kv, g_k,
           ln_mix_b, w_q, g_q, w_out_b, ln_mlp, w_up, w_down, ln_ple, w_ple_gate, w_ple_proj):
    batch, seq, d = x.shape
    depth = p.shape[0]
    n_a = ln_mix_a.shape[0]
    m = batch * seq
    xf = x.reshape(m, d)
    pf = p.reshape(depth, m, p.shape[-1])
    rows = lambda g: g.reshape(g.shape[0], 1, g.shape[1])
    pairs = lambda g: jnp.concatenate([g, g], axis=-1).reshape(g.shape[0], 1, LANES)
    ln_mix_a, g_v_a, ln_mix_b, ln_mlp, ln_ple = map(rows, (ln_mix_a, g_v_a, ln_mix_b, ln_mlp, ln_ple))
    w_ple_proj = w_ple_proj.astype(_BF16)
    bias = jnp.repeat(jnp.swapaxes(b_spatial, 1, 2), d // b_spatial.shape[1], axis=2)

    def post_weights(i):
        out = (w_out_a, i) if i < n_a else (w_out_b, i - n_a)
        return {("out", i): out, ("up", i): (w_up, i), ("down", i): (w_down, i),
                ("gate", i): (w_ple_gate, i)}

    cast = {}

    def bf16_weight(name, source):
        if name in cast:
            return cast.pop(name)[None], 0
        stacked, layer = source
        return stacked.astype(_BF16), layer

    def ride(call, wanted):
        names = list(wanted)
        main, *casts = call([wanted[n] for n in names])
        cast.update(zip(names, casts))
        return main

    def q_group(j):
        return dict(params=[(ln_mix_b, j), bf16_weight(("q", j), (w_q, j)), (pairs(g_q), j)],
                    n_norm=d // LANES, out_scale=-LOG2_E * HEAD_DIM ** -0.5, segment_major=False)

    kv_shared = q = None
    for i in range(depth):
        if i < n_a:
            mixer_params = [(ln_mix_a, i), (w_in_a, i), (g_v_a, i), (w_spatial, i), (bias, i)]
            wanted = post_weights(i) if i == 0 else {}
            y = ride(lambda riders: _sgu_call(xf, mixer_params, riders), wanted)
        else:
            if q is None:
                q, = _proj_call(xf, [q_group(i - n_a)])
            y = _attn_call(q, kv_shared, batch, seq)
        weights = {n: bf16_weight(n, src) for n, src in post_weights(i).items()}
        post_params = [weights[("out", i)], (ln_mlp, i), weights[("up", i)], weights[("down", i)],
                       (ln_ple, i), weights[("gate", i)], (w_ple_proj, i)]
        wanted = post_weights(i + 1) if i + 1 < depth else {}
        if i == n_a - 1:
            wanted[("kv", 0)] = (w_kv[None], 0)
        if n_a <= i + 1 < depth:
            wanted[("q", i + 1 - n_a)] = (w_q, i + 1 - n_a)
        xf = ride(lambda riders: _post_call(xf, y, (pf, i), post_params, riders), wanted)
        groups = []
        if i == n_a - 1:
            groups.append(dict(
                params=[(rows(ln_kv[None]), 0), bf16_weight(("kv", 0), (w_kv[None], 0)),
                        (pairs(g_k[None]), 0)],
                n_norm=d // LANES, out_scale=1.0, segment_major=True))
        if n_a <= i + 1 < depth:
            groups.append(q_group(i + 1 - n_a))
        projected = list(_proj_call(xf, groups)) if groups else []
        if i == n_a - 1:
            kv_shared = projected.pop(0)
        q = projected.pop(0) if projected else None
    return xf.reshape(batch, seq, d)
```

```python
import functools

import jax
import jax.numpy as jnp
from jax import lax
from jax.experimental import pallas as pl
from jax.experimental.pallas import tpu as pltpu

EPS = 1e-6
CHUNK = 128
HEAD_DIM = 64
LANES = 128
MXU_WIDTH = 256
BF16_ROWS = 16
TOKEN_TILE = 512
WIDE_TOKEN_TILE = 1024
ATTN_BLOCK = 128
STATIC_SWEEP = 3
assert STATIC_SWEEP >= 2
FF_CHUNK = 1024
VMEM_LIMIT = 56 * 1024 * 1024
SUBLANES = 8
KEY_SEGMENT = ATTN_BLOCK // SUBLANES
LOG2_E = 1.4426950408889634
WEIGHT_FLOOR = 2.0 ** -126

_BF16 = jnp.bfloat16
_F32 = jnp.float32


def _rms(x, g):
    return x * lax.rsqrt(jnp.mean(x * x, axis=-1, keepdims=True) + EPS) * g


def _gelu_tanh(x):
    c = -2.0 * LOG2_E * (2.0 / jnp.pi) ** 0.5
    return x / (1.0 + jnp.exp2(x * (x * x * (c * 0.044715) + c)))


def _layer_spec(param):
    stacked, layer = param
    rest = stacked.shape[1:]
    return pl.BlockSpec((None,) + rest, lambda *_: (layer,) + (0,) * len(rest),
                        pipeline_mode=pl.Buffered(1))


def _rider_specs(riders, n_steps):
    arrays, in_specs, out_specs, out_shapes = [], [], [], []
    for stacked, layer in riders:
        _, r, c = stacked.shape
        rb = r // n_steps
        assert rb * n_steps == r and rb % BF16_ROWS == 0, (stacked.shape, n_steps)
        arrays.append(stacked)
        in_specs.append(pl.BlockSpec((None, rb, c), lambda i, layer=layer: (layer, i, 0)))
        out_specs.append(pl.BlockSpec((rb, c), lambda i: (i, 0)))
        out_shapes.append(jax.ShapeDtypeStruct((r, c), _BF16))
    return arrays, in_specs, out_specs, out_shapes


def _cast_riders(src_refs, dst_refs):
    for src, dst in zip(src_refs, dst_refs):
        dst[...] = src[...].astype(dst.dtype)


def _params(n_axes):
    return pltpu.CompilerParams(
        dimension_semantics=("arbitrary",) * n_axes, vmem_limit_bytes=VMEM_LIMIT)


def _sgu_kernel(x_ref, ln_ref, win_ref, gv_ref, ws_ref, bias_ref, *rest, n_riders):
    y_ref = rest[n_riders]
    _cast_riders(rest[:n_riders], rest[n_riders + 1:])
    tm, d = x_ref.shape
    n_groups = ws_ref.shape[0]
    n_chunks = tm // CHUNK
    h = _rms(x_ref[...], ln_ref[...]).astype(_BF16)
    v = _gelu_tanh(jnp.dot(h, win_ref[:, d:].astype(_BF16), preferred_element_type=_F32))
    v = _rms(v, gv_ref[...]).astype(_BF16)
    u_cols = d // MXU_WIDTH
    u_tiles = [_gelu_tanh(jnp.dot(h, win_ref[:, c * MXU_WIDTH:(c + 1) * MXU_WIDTH].astype(_BF16),
                                  preferred_element_type=_F32)) for c in range(u_cols - 1)]
    row = lax.broadcasted_iota(jnp.int32, (CHUNK, CHUNK), 0)
    col = lax.broadcasted_iota(jnp.int32, (CHUNK, CHUNK), 1)
    causal = col <= row
    mix_cols = []
    for g in range(n_groups):
        w = jnp.where(causal, ws_ref[g], 0.0).astype(_BF16)
        vg = jnp.concatenate(
            [v[c * CHUNK:(c + 1) * CHUNK, g * LANES:(g + 1) * LANES] for c in range(n_chunks)],
            axis=1)
        mg = jnp.dot(w, vg, preferred_element_type=_F32)
        mix_cols.append(jnp.concatenate(
            [mg[:, c * LANES:(c + 1) * LANES] for c in range(n_chunks)], axis=0))
    mix = jnp.concatenate(mix_cols, axis=1)
    bias = jnp.concatenate([bias_ref[...]] * n_chunks, axis=0)
    u_tiles.append(_gelu_tanh(jnp.dot(h, win_ref[:, d - MXU_WIDTH:d].astype(_BF16),
                                      preferred_element_type=_F32)))
    for c, u in enumerate(u_tiles):
        cols = slice(c * MXU_WIDTH, (c + 1) * MXU_WIDTH)
        y_ref[:, cols] = (u * (mix[:, cols] + bias[:, cols])).astype(y_ref.dtype)


def _sgu_call(x, params, riders=()):
    m, d = x.shape
    tm = WIDE_TOKEN_TILE
    tile = pl.BlockSpec((tm, d), lambda i: (i, 0))
    r_arrays, r_in, r_out, r_shapes = _rider_specs(riders, m // tm)
    return pl.pallas_call(
        functools.partial(_sgu_kernel, n_riders=len(riders)),
        grid=(m // tm,),
        in_specs=[tile] + [_layer_spec(a) for a in params] + r_in,
        out_specs=[tile] + r_out,
        out_shape=[jax.ShapeDtypeStruct((m, d), _BF16)] + r_shapes,
        compiler_params=_params(1),
        name="sgu_mixer",
    )(x, *[a for a, _ in params], *r_arrays)


def _post_kernel(x_ref, y_ref, p_ref, wo_ref, lnm_ref, wup_ref, wdn_ref,
                 lnp_ref, wg_ref, wp_ref, *rest, n_riders):
    o_ref = rest[n_riders]
    _cast_riders(rest[:n_riders], rest[n_riders + 1:])
    d_ff = wup_ref.shape[1]
    x = x_ref[...] + jnp.dot(y_ref[...], wo_ref[...], preferred_element_type=_F32)
    h = _rms(x, lnm_ref[...]).astype(_BF16)
    acc = x
    for c in range(d_ff // FF_CHUNK):
        a = jnp.dot(h, wup_ref[:, c * FF_CHUNK:(c + 1) * FF_CHUNK], preferred_element_type=_F32)
        a = jnp.maximum(a, 0.0)
        a = (a * a).astype(_BF16)
        acc = acc + jnp.dot(a, wdn_ref[c * FF_CHUNK:(c + 1) * FF_CHUNK, :],
                            preferred_element_type=_F32)
    x = acc
    hg = _rms(x, lnp_ref[...]).astype(_BF16)
    gate = 1.0 / (1.0 + jnp.exp(-jnp.dot(hg, wg_ref[...], preferred_element_type=_F32)))
    pp = jnp.dot(p_ref[...].astype(_BF16), wp_ref[...], preferred_element_type=_F32)
    o_ref[...] = x + pp * gate


def _post_call(x, y, p_param, params, riders=()):
    m, d = x.shape
    tm = TOKEN_TILE
    tile = pl.BlockSpec((tm, d), lambda i: (i, 0))
    p, p_layer = p_param
    r_arrays, r_in, r_out, r_shapes = _rider_specs(riders, m // tm)
    return pl.pallas_call(
        functools.partial(_post_kernel, n_riders=len(riders)),
        grid=(m // tm,),
        in_specs=[tile, tile, pl.BlockSpec((None, tm, p.shape[2]), lambda i: (p_layer, i, 0))]
        + [_layer_spec(a) for a in params] + r_in,
        out_specs=[tile] + r_out,
        out_shape=[jax.ShapeDtypeStruct((m, d), _F32)] + r_shapes,
        compiler_params=_params(1),
        name="post_mlp_ple",
    )(x, y, p, *[a for a, _ in params], *r_arrays)


def _proj_kernel(x_ref, *refs, groups):
    x = x_ref[...]
    for n, cfg in enumerate(groups):
        _project_heads(x, *refs[3 * n:3 * n + 3], refs[3 * len(groups) + n], **cfg)


def _project_heads(x, ln_ref, w_ref, g_ref, o_ref, *, n_norm, out_scale, segment_major):
    tm = x.shape[0]
    h = _rms(x, ln_ref[...]).astype(_BF16)
    if segment_major:
        r = lax.broadcasted_iota(jnp.int32, (ATTN_BLOCK, ATTN_BLOCK), 0)
        c = lax.broadcasted_iota(jnp.int32, (ATTN_BLOCK, ATTN_BLOCK), 1)
        src = KEY_SEGMENT * (r & (SUBLANES - 1)) + r // SUBLANES
        perm = jnp.where(c == src, 1.0, 0.0).astype(_BF16)
        h = jnp.concatenate(
            [jnp.dot(perm, h[b * ATTN_BLOCK:(b + 1) * ATTN_BLOCK],
                     preferred_element_type=_F32).astype(_BF16)
             for b in range(tm // ATTN_BLOCK)], axis=0)
    first_head = lax.broadcasted_iota(jnp.int32, (tm, LANES), 1) < HEAD_DIM
    gain = g_ref[...] * out_scale
    for c in range(w_ref.shape[1] // MXU_WIDTH):
        y = jnp.dot(h, w_ref[:, c * MXU_WIDTH:(c + 1) * MXU_WIDTH], preferred_element_type=_F32)
        for t in range(MXU_WIDTH // LANES):
            blk = y[:, t * LANES:(t + 1) * LANES]
            col = c * MXU_WIDTH + t * LANES
            if col < n_norm * LANES:
                sq = blk * blk
                s_a = jnp.sum(jnp.where(first_head, sq, 0.0), axis=-1, keepdims=True)
                s_b = jnp.sum(jnp.where(first_head, 0.0, sq), axis=-1, keepdims=True)
                ms = jnp.where(first_head, s_a, s_b) * (1.0 / HEAD_DIM)
                blk = blk * (lax.rsqrt(ms + EPS) * gain)
            o_ref[:, col:col + LANES] = blk.astype(o_ref.dtype)


def _proj_call(x, groups):
    m, d = x.shape
    tm = WIDE_TOKEN_TILE
    tile = lambda width: pl.BlockSpec((tm, width), lambda i: (i, 0))
    params = [a for g in groups for a in g["params"]]
    widths = [g["params"][1][0].shape[2] for g in groups]
    cfgs = tuple({k: g[k] for k in ("n_norm", "out_scale", "segment_major")} for g in groups)
    return pl.pallas_call(
        functools.partial(_proj_kernel, groups=cfgs),
        grid=(m // tm,),
        in_specs=[tile(d)] + [_layer_spec(a) for a in params],
        out_specs=[tile(n) for n in widths],
        out_shape=[jax.ShapeDtypeStruct((m, n), _BF16) for n in widths],
        compiler_params=_params(1),
        name="head_proj",
    )(x, *[a for a, _ in params])


def _attn_kernel(q_ref, k_ref, v_ref, o_ref, qs_ref, passed_ref, acc_ref, bound_ref):
    blk = ATTN_BLOCK
    n_pairs = qs_ref.shape[0]
    n_groups = blk // SUBLANES
    lane = lax.broadcasted_iota(jnp.int32, (blk, LANES), 1)
    first_head = lane < HEAD_DIM
    sub = lax.broadcasted_iota(jnp.int32, (SUBLANES, 2 * blk), 0)

    def block_weights(z, passed, valid):
        beta = 1.0 / (1.0 + jnp.exp2(z))
        keep = 1.0 - beta
        if valid is not None:
            keep = jnp.where(valid, keep, 1.0)
        after = [None] * n_groups
        run = None
        for v in range(n_groups - 1, -1, -1):
            after[v] = run
            kv = keep[v * SUBLANES:(v + 1) * SUBLANES]
            run = kv if run is None else run * kv
        later = jnp.where(sub + 1 < SUBLANES, pltpu.roll(run, SUBLANES - 1, 0), 1.0)
        for step in (1, 2, 4):
            shifted = pltpu.roll(later, SUBLANES - step, 0)
            later = later * jnp.where(sub + step < SUBLANES, shifted, 1.0)
        whole = later * run
        base = later if passed is None else later * passed
        parts = []
        for v in range(n_groups):
            scale = base if after[v] is None else after[v] * base
            parts.append(beta[v * SUBLANES:(v + 1) * SUBLANES] * scale)
        w = jnp.concatenate(parts, axis=0)
        if valid is not None:
            w = jnp.where(valid, w, 0.0)
        total = jnp.broadcast_to(whole[0:1], whole.shape)
        return w.astype(_BF16), total if passed is None else passed * total

    def sweep(j_last, n_blocks, first, slot):
        rows = pl.ds(pl.multiple_of((j_last - (n_blocks - 1)) * blk, blk), n_blocks * blk)
        valid = None
        if first:
            r = lax.broadcasted_iota(jnp.int32, (blk, 2 * blk), 0)
            key = KEY_SEGMENT * (r & (SUBLANES - 1)) + r // SUBLANES
            query = lax.broadcasted_iota(jnp.int32, (blk, 2 * blk), 1) & (blk - 1)
            valid = key < query
        pairs = range(n_pairs)
        zs = [lax.dot_general(k_ref[rows, p * LANES:(p + 1) * LANES], qs_ref[p],
                              (((1,), (1,)), ((), ())), preferred_element_type=_F32)
              for p in pairs]
        weights = []
        top = None
        for p in pairs:
            passed = None if first else passed_ref[p]
            parts = [None] * n_blocks
            for b in range(n_blocks - 1, -1, -1):
                parts[b], passed = block_weights(
                    zs[p][b * blk:(b + 1) * blk], passed,
                    valid if first and b == n_blocks - 1 else None)
            passed_ref[p] = passed
            top = passed if top is None else jnp.maximum(top, passed)
            weights.append(jnp.concatenate(parts, axis=0) if n_blocks > 1 else parts[0])
        bound_ref[0] = jnp.max(top)
        for p in pairs:
            acc = lax.dot_general(v_ref[rows, p * LANES:(p + 1) * LANES], weights[p],
                                  (((0,), (0,)), ((), ())), preferred_element_type=_F32)
            acc_ref[slot, p] = acc if first else acc_ref[slot, p] + acc

    def write_out(i, slot):
        rows = pl.ds(pl.multiple_of(i * blk, blk), blk)
        for p in range(n_pairs):
            acc = acc_ref[slot, p]
            out_t = jnp.concatenate([acc[:HEAD_DIM, :blk], acc[HEAD_DIM:, blk:]], axis=0)
            o_ref[rows, p * LANES:(p + 1) * LANES] = out_t.T.astype(o_ref.dtype)

    def query_block(i, carry):
        rows = pl.ds(pl.multiple_of(i * blk, blk), blk)
        slot = i & 1
        for p in range(n_pairs):
            q2 = q_ref[rows, p * LANES:(p + 1) * LANES]
            zeros = jnp.zeros_like(q2)
            qs_ref[p] = jnp.concatenate(
                [jnp.where(first_head, q2, zeros), jnp.where(first_head, zeros, q2)], axis=0)

        for n in range(1, STATIC_SWEEP + 1):
            @pl.when((i == n - 1) if n < STATIC_SWEEP else (i >= n - 1))
            def _(n=n):
                if n > 1:
                    write_out(i - 1, 1 - slot)
                sweep(i, n, first=True, slot=slot)

        def cond(state):
            j, top = state
            return jnp.logical_and(j >= 0, top >= WEIGHT_FLOOR)

        def body(state):
            j, _ = state
            sweep(j, 1, first=False, slot=slot)
            return j - 1, bound_ref[0]

        lax.while_loop(cond, body, (i - STATIC_SWEEP, bound_ref[0]))
        return carry

    n_q = q_ref.shape[0] // blk
    lax.fori_loop(0, n_q, query_block, 0)
    write_out(n_q - 1, (n_q - 1) & 1)


def _attn_call(q, kv, batch, seq):
    m, d = q.shape
    blk = ATTN_BLOCK
    n_pairs = d // LANES
    return pl.pallas_call(
        _attn_kernel,
        grid=(batch,),
        in_specs=[
            pl.BlockSpec((seq, d), lambda b: (b, 0)),
            pl.BlockSpec((seq, d), lambda b: (b, 0)),
            pl.BlockSpec((seq, d), lambda b: (b, 1)),
        ],
        out_specs=pl.BlockSpec((seq, d), lambda b: (b, 0)),
        out_shape=jax.ShapeDtypeStruct((m, d), _BF16),
        scratch_shapes=[
            pltpu.VMEM((n_pairs, 2 * blk, LANES), _BF16),
            pltpu.VMEM((n_pairs, SUBLANES, 2 * blk), _F32),
            pltpu.VMEM((2, n_pairs, LANES, 2 * blk), _F32),
            pltpu.SMEM((1,), _F32),
        ],
        compiler_params=_params(1),
        name="stick_breaking",
    )(q, kv, kv)


def kernel(x, p, ln_mix_a, w_in_a, g_v_a, w_spatial, b_spatial, w_out_a, ln_kv, w_kv, g_k,
           ln_mix_b, w_q, g_q, w_out_b, ln_mlp, w_up, w_down, ln_ple, w_ple_gate, w_ple_proj):
    batch, seq, d = x.shape
    depth = p.shape[0]
    n_a = ln_mix_a.shape[0]
    m = batch * seq
    xf = x.reshape(m, d)
    pf = p.reshape(depth, m, p.shape[-1])
    rows = lambda g: g.reshape(g.shape[0], 1, g.shape[1])
    pairs = lambda g: jnp.concatenate([g, g], axis=-1).reshape(g.shape[0], 1, LANES)
    ln_mix_a, g_v_a, ln_mix_b, ln_mlp, ln_ple = map(rows, (ln_mix_a, g_v_a, ln_mix_b, ln_mlp, ln_ple))
    w_ple_proj = w_ple_proj.astype(_BF16)
    bias = jnp.repeat(jnp.swapaxes(b_spatial, 1, 2), d // b_spatial.shape[1], axis=2)

    def post_weights(i):
        out = (w_out_a, i) if i < n_a else (w_out_b, i - n_a)
        return {("out", i): out, ("up", i): (w_up, i), ("down", i): (w_down, i),
                ("gate", i): (w_ple_gate, i)}

    cast = {}

    def bf16_weight(name, source):
        if name in cast:
            return cast.pop(name)[None], 0
        stacked, layer = source
        return stacked.astype(_BF16), layer

    def ride(call, wanted):
        names = list(wanted)
        main, *casts = call([wanted[n] for n in names])
        cast.update(zip(names, casts))
        return main

    def q_group(j):
        return dict(params=[(ln_mix_b, j), bf16_weight(("q", j), (w_q, j)), (pairs(g_q), j)],
                    n_norm=d // LANES, out_scale=-LOG2_E * HEAD_DIM ** -0.5, segment_major=False)

    kv_shared = q = None
    for i in range(depth):
        if i < n_a:
            mixer_params = [(ln_mix_a, i), (w_in_a, i), (g_v_a, i), (w_spatial, i), (bias, i)]
            wanted = post_weights(i) if i == 0 else {}
            y = ride(lambda riders: _sgu_call(xf, mixer_params, riders), wanted)
        else:
            if q is None:
                q, = _proj_call(xf, [q_group(i - n_a)])
            y = _attn_call(q, kv_shared, batch, seq)
        weights = {n: bf16_weight(n, src) for n, src in post_weights(i).items()}
        post_params = [weights[("out", i)], (ln_mlp, i), weights[("up", i)], weights[("down", i)],
                       (ln_ple, i), weights[("gate", i)], (w_ple_proj, i)]
        wanted = post_weights(i + 1) if i + 1 < depth else {}
        if i == n_a - 1:
            wanted[("kv", 0)] = (w_kv[None], 0)
        if n_a <= i + 1 < depth:
            wanted[("q", i + 1 - n_a)] = (w_q, i + 1 - n_a)
        xf = ride(lambda riders: _post_call(xf, y, (pf, i), post_params, riders), wanted)
        groups = []
        if n_a <= i + 1 < depth:
            groups.append(q_group(i + 1 - n_a))
        if i == n_a - 1:
            groups.append(dict(
                params=[(rows(ln_kv[None]), 0), bf16_weight(("kv", 0), (w_kv[None], 0)),
                        (pairs(g_k[None]), 0)],
                n_norm=d // LANES, out_scale=1.0, segment_major=True))
        projected = list(_proj_call(xf, groups)) if groups else []
        q = projected.pop(0) if n_a <= i + 1 < depth else None
        if i == n_a - 1:
            kv_shared = projected.pop(0)
    return xf.reshape(batch, seq, d)
```

```python
import functools

import jax
import jax.numpy as jnp
from jax import lax
from jax.experimental import pallas as pl
from jax.experimental.pallas import tpu as pltpu

EPS = 1e-6
CHUNK = 128
HEAD_DIM = 64
LANES = 128
MXU_WIDTH = 256
BF16_ROWS = 16
TOKEN_TILE = 512
WIDE_TOKEN_TILE = 1024
ATTN_BLOCK = 128
STATIC_SWEEP = 3
assert STATIC_SWEEP >= 2
FF_CHUNK = 1024
VMEM_LIMIT = 56 * 1024 * 1024
SUBLANES = 8
KEY_SEGMENT = ATTN_BLOCK // SUBLANES
LOG2_E = 1.4426950408889634
WEIGHT_FLOOR = 2.0 ** -126

_BF16 = jnp.bfloat16
_F32 = jnp.float32


def _rms(x, g):
    return x * lax.rsqrt(jnp.mean(x * x, axis=-1, keepdims=True) + EPS) * g


def _gelu_tanh(x):
    c = -2.0 * LOG2_E * (2.0 / jnp.pi) ** 0.5
    return x / (1.0 + jnp.exp2(x * (x * x * (c * 0.044715) + c)))


def _layer_spec(param):
    stacked, layer = param
    rest = stacked.shape[1:]
    return pl.BlockSpec((None,) + rest, lambda *_: (layer,) + (0,) * len(rest),
                        pipeline_mode=pl.Buffered(1))


def _rider_specs(riders, n_steps):
    arrays, in_specs, out_specs, out_shapes = [], [], [], []
    for stacked, layer in riders:
        _, r, c = stacked.shape
        rb = r // n_steps
        assert rb * n_steps == r and rb % BF16_ROWS == 0, (stacked.shape, n_steps)
        arrays.append(stacked)
        in_specs.append(pl.BlockSpec((None, rb, c), lambda i, layer=layer: (layer, i, 0)))
        out_specs.append(pl.BlockSpec((rb, c), lambda i: (i, 0)))
        out_shapes.append(jax.ShapeDtypeStruct((r, c), _BF16))
    return arrays, in_specs, out_specs, out_shapes


def _cast_riders(src_refs, dst_refs):
    for src, dst in zip(src_refs, dst_refs):
        dst[...] = src[...].astype(dst.dtype)


def _params(n_axes):
    return pltpu.CompilerParams(
        dimension_semantics=("arbitrary",) * n_axes, vmem_limit_bytes=VMEM_LIMIT)


def _sgu_kernel(x_ref, ln_ref, win_ref, gv_ref, ws_ref, bias_ref, *rest, n_riders):
    y_ref = rest[n_riders]
    _cast_riders(rest[:n_riders], rest[n_riders + 1:])
    tm, d = x_ref.shape
    n_groups = ws_ref.shape[0]
    n_chunks = tm // CHUNK
    h = _rms(x_ref[...], ln_ref[...]).astype(_BF16)
    v = _gelu_tanh(jnp.dot(h, win_ref[:, d:].astype(_BF16), preferred_element_type=_F32))
    v = _rms(v, gv_ref[...]).astype(_BF16)
    u_cols = d // MXU_WIDTH
    u_tiles = [_gelu_tanh(jnp.dot(h, win_ref[:, c * MXU_WIDTH:(c + 1) * MXU_WIDTH].astype(_BF16),
                                  preferred_element_type=_F32)) for c in range(u_cols - 1)]
    row = lax.broadcasted_iota(jnp.int32, (CHUNK, CHUNK), 0)
    col = lax.broadcasted_iota(jnp.int32, (CHUNK, CHUNK), 1)
    causal = col <= row
    mix_cols = []
    for g in range(n_groups):
        w = jnp.where(causal, ws_ref[g], 0.0).astype(_BF16)
        vg = jnp.concatenate(
            [v[c * CHUNK:(c + 1) * CHUNK, g * LANES:(g + 1) * LANES] for c in range(n_chunks)],
            axis=1)
        mg = jnp.dot(w, vg, preferred_element_type=_F32)
        mix_cols.append(jnp.concatenate(
            [mg[:, c * LANES:(c + 1) * LANES] for c in range(n_chunks)], axis=0))
    mix = jnp.concatenate(mix_cols, axis=1)
    bias = jnp.concatenate([bias_ref[...]] * n_chunks, axis=0)
    u_tiles.append(_gelu_tanh(jnp.dot(h, win_ref[:, d - MXU_WIDTH:d].astype(_BF16),
                                      preferred_element_type=_F32)))
    for c, u in enumerate(u_tiles):
        cols = slice(c * MXU_WIDTH, (c + 1) * MXU_WIDTH)
        y_ref[:, cols] = (u * (mix[:, cols] + bias[:, cols])).astype(y_ref.dtype)


def _sgu_call(x, params, riders=()):
    m, d = x.shape
    tm = WIDE_TOKEN_TILE
    tile = pl.BlockSpec((tm, d), lambda i: (i, 0))
    r_arrays, r_in, r_out, r_shapes = _rider_specs(riders, m // tm)
    return pl.pallas_call(
        functools.partial(_sgu_kernel, n_riders=len(riders)),
        grid=(m // tm,),
        in_specs=[tile] + [_layer_spec(a) for a in params] + r_in,
        out_specs=[tile] + r_out,
        out_shape=[jax.ShapeDtypeStruct((m, d), _BF16)] + r_shapes,
        compiler_params=_params(1),
        name="sgu_mixer",
    )(x, *[a for a, _ in params], *r_arrays)


def _post_kernel(x_ref, y_ref, p_ref, wo_ref, lnm_ref, wup_ref, wdn_ref,
                 lnp_ref, wg_ref, wp_ref, *rest, n_riders):
    o_ref = rest[n_riders]
    _cast_riders(rest[:n_riders], rest[n_riders + 1:])
    d_ff = wup_ref.shape[1]
    x = x_ref[...] + jnp.dot(y_ref[...], wo_ref[...], preferred_element_type=_F32)
    h = _rms(x, lnm_ref[...]).astype(_BF16)
    acc = x
    for c in range(d_ff // FF_CHUNK):
        a = jnp.dot(h, wup_ref[:, c * FF_CHUNK:(c + 1) * FF_CHUNK], preferred_element_type=_F32)
        a = jnp.maximum(a, 0.0)
        a = (a * a).astype(_BF16)
        acc = acc + jnp.dot(a, wdn_ref[c * FF_CHUNK:(c + 1) * FF_CHUNK, :],
                            preferred_element_type=_F32)
    x = acc
    hg = _rms(x, lnp_ref[...]).astype(_BF16)
    gate = 1.0 / (1.0 + jnp.exp(-jnp.dot(hg, wg_ref[...], preferred_element_type=_F32)))
    pp = jnp.dot(p_ref[...].astype(_BF16), wp_ref[...], preferred_element_type=_F32)
    o_ref[...] = x + pp * gate


def _post_call(x, y, p_param, params, riders=()):
    m, d = x.shape
    tm = TOKEN_TILE if riders else WIDE_TOKEN_TILE
    tile = pl.BlockSpec((tm, d), lambda i: (i, 0))
    p, p_layer = p_param
    r_arrays, r_in, r_out, r_shapes = _rider_specs(riders, m // tm)
    return pl.pallas_call(
        functools.partial(_post_kernel, n_riders=len(riders)),
        grid=(m // tm,),
        in_specs=[tile, tile, pl.BlockSpec((None, tm, p.shape[2]), lambda i: (p_layer, i, 0))]
        + [_layer_spec(a) for a in params] + r_in,
        out_specs=[tile] + r_out,
        out_shape=[jax.ShapeDtypeStruct((m, d), _F32)] + r_shapes,
        compiler_params=_params(1),
        name="post_mlp_ple",
    )(x, y, p, *[a for a, _ in params], *r_arrays)


def _proj_kernel(x_ref, *refs, groups):
    x = x_ref[...]
    for n, cfg in enumerate(groups):
        _project_heads(x, *refs[3 * n:3 * n + 3], refs[3 * len(groups) + n], **cfg)


def _project_heads(x, ln_ref, w_ref, g_ref, o_ref, *, n_norm, out_scale, segment_major):
    tm = x.shape[0]
    h = _rms(x, ln_ref[...]).astype(_BF16)
    if segment_major:
        r = lax.broadcasted_iota(jnp.int32, (ATTN_BLOCK, ATTN_BLOCK), 0)
        c = lax.broadcasted_iota(jnp.int32, (ATTN_BLOCK, ATTN_BLOCK), 1)
        src = KEY_SEGMENT * (r & (SUBLANES - 1)) + r // SUBLANES
        perm = jnp.where(c == src, 1.0, 0.0).astype(_BF16)
        h = jnp.concatenate(
            [jnp.dot(perm, h[b * ATTN_BLOCK:(b + 1) * ATTN_BLOCK],
                     preferred_element_type=_F32).astype(_BF16)
             for b in range(tm // ATTN_BLOCK)], axis=0)
    first_head = lax.broadcasted_iota(jnp.int32, (tm, LANES), 1) < HEAD_DIM
    gain = g_ref[...] * out_scale
    for c in range(w_ref.shape[1] // MXU_WIDTH):
        y = jnp.dot(h, w_ref[:, c * MXU_WIDTH:(c + 1) * MXU_WIDTH], preferred_element_type=_F32)
        for t in range(MXU_WIDTH // LANES):
            blk = y[:, t * LANES:(t + 1) * LANES]
            col = c * MXU_WIDTH + t * LANES
            if col < n_norm * LANES:
                sq = blk * blk
                s_a = jnp.sum(jnp.where(first_head, sq, 0.0), axis=-1, keepdims=True)
                s_b = jnp.sum(jnp.where(first_head, 0.0, sq), axis=-1, keepdims=True)
                ms = jnp.where(first_head, s_a, s_b) * (1.0 / HEAD_DIM)
                blk = blk * (lax.rsqrt(ms + EPS) * gain)
            o_ref[:, col:col + LANES] = blk.astype(o_ref.dtype)


def _proj_call(x, groups):
    m, d = x.shape
    tm = WIDE_TOKEN_TILE
    tile = lambda width: pl.BlockSpec((tm, width), lambda i: (i, 0))
    params = [a for g in groups for a in g["params"]]
    widths = [g["params"][1][0].shape[2] for g in groups]
    cfgs = tuple({k: g[k] for k in ("n_norm", "out_scale", "segment_major")} for g in groups)
    return pl.pallas_call(
        functools.partial(_proj_kernel, groups=cfgs),
        grid=(m // tm,),
        in_specs=[tile(d)] + [_layer_spec(a) for a in params],
        out_specs=[tile(n) for n in widths],
        out_shape=[jax.ShapeDtypeStruct((m, n), _BF16) for n in widths],
        compiler_params=_params(1),
        name="head_proj",
    )(x, *[a for a, _ in params])


def _attn_kernel(q_ref, k_ref, v_ref, o_ref, qs_ref, passed_ref, acc_ref, bound_ref):
    blk = ATTN_BLOCK
    n_pairs = qs_ref.shape[0]
    n_groups = blk // SUBLANES
    lane = lax.broadcasted_iota(jnp.int32, (blk, LANES), 1)
    first_head = lane < HEAD_DIM
    sub = lax.broadcasted_iota(jnp.int32, (SUBLANES, 2 * blk), 0)

    def block_weights(z, passed, valid):
        beta = 1.0 / (1.0 + jnp.exp2(z))
        keep = 1.0 - beta
        if valid is not None:
            keep = jnp.where(valid, keep, 1.0)
        after = [None] * n_groups
        run = None
        for v in range(n_groups - 1, -1, -1):
            after[v] = run
            kv = keep[v * SUBLANES:(v + 1) * SUBLANES]
            run = kv if run is None else run * kv
        later = jnp.where(sub + 1 < SUBLANES, pltpu.roll(run, SUBLANES - 1, 0), 1.0)
        for step in (1, 2, 4):
            shifted = pltpu.roll(later, SUBLANES - step, 0)
            later = later * jnp.where(sub + step < SUBLANES, shifted, 1.0)
        whole = later * run
        base = later if passed is None else later * passed
        parts = []
        for v in range(n_groups):
            scale = base if after[v] is None else after[v] * base
            parts.append(beta[v * SUBLANES:(v + 1) * SUBLANES] * scale)
        w = jnp.concatenate(parts, axis=0)
        if valid is not None:
            w = jnp.where(valid, w, 0.0)
        total = jnp.broadcast_to(whole[0:1], whole.shape)
        return w.astype(_BF16), total if passed is None else passed * total

    def sweep(j_last, n_blocks, first, slot):
        rows = pl.ds(pl.multiple_of((j_last - (n_blocks - 1)) * blk, blk), n_blocks * blk)
        valid = None
        if first:
            r = lax.broadcasted_iota(jnp.int32, (blk, 2 * blk), 0)
            key = KEY_SEGMENT * (r & (SUBLANES - 1)) + r // SUBLANES
            query = lax.broadcasted_iota(jnp.int32, (blk, 2 * blk), 1) & (blk - 1)
            valid = key < query
        pairs = range(n_pairs)
        zs = [lax.dot_general(k_ref[rows, p * LANES:(p + 1) * LANES], qs_ref[p],
                              (((1,), (1,)), ((), ())), preferred_element_type=_F32)
              for p in pairs]
        weights = []
        top = None
        for p in pairs:
            passed = None if first else passed_ref[p]
            parts = [None] * n_blocks
            for b in range(n_blocks - 1, -1, -1):
                parts[b], passed = block_weights(
                    zs[p][b * blk:(b + 1) * blk], passed,
                    valid if first and b == n_blocks - 1 else None)
            passed_ref[p] = passed
            top = passed if top is None else jnp.maximum(top, passed)
            weights.append(jnp.concatenate(parts, axis=0) if n_blocks > 1 else parts[0])
        bound_ref[0] = jnp.max(top)
        for p in pairs:
            acc = lax.dot_general(v_ref[rows, p * LANES:(p + 1) * LANES], weights[p],
                                  (((0,), (0,)), ((), ())), preferred_element_type=_F32)
            acc_ref[slot, p] = acc if first else acc_ref[slot, p] + acc

    def write_out(i, slot):
        rows = pl.ds(pl.multiple_of(i * blk, blk), blk)
        for p in range(n_pairs):
            acc = acc_ref[slot, p]
            out_t = jnp.concatenate([acc[:HEAD_DIM, :blk], acc[HEAD_DIM:, blk:]], axis=0)
            o_ref[rows, p * LANES:(p + 1) * LANES] = out_t.T.astype(o_ref.dtype)

    def query_block(i, carry):
        rows = pl.ds(pl.multiple_of(i * blk, blk), blk)
        slot = i & 1
        for p in range(n_pairs):
            q2 = q_ref[rows, p * LANES:(p + 1) * LANES]
            zeros = jnp.zeros_like(q2)
            qs_ref[p] = jnp.concatenate(
                [jnp.where(first_head, q2, zeros), jnp.where(first_head, zeros, q2)], axis=0)

        for n in range(1, STATIC_SWEEP + 1):
            @pl.when((i == n - 1) if n < STATIC_SWEEP else (i >= n - 1))
            def _(n=n):
                if n > 1:
                    write_out(i - 1, 1 - slot)
                sweep(i, n, first=True, slot=slot)

        def cond(state):
            j, top = state
            return jnp.logical_and(j >= 0, top >= WEIGHT_FLOOR)

        def body(state):
            j, _ = state
            sweep(j, 1, first=False, slot=slot)
            return j - 1, bound_ref[0]

        lax.while_loop(cond, body, (i - STATIC_SWEEP, bound_ref[0]))
        return carry

    n_q = q_ref.shape[0] // blk
    lax.fori_loop(0, n_q, query_block, 0)
    write_out(n_q - 1, (n_q - 1) & 1)


def _attn_call(q, kv, batch, seq):
    m, d = q.shape
    blk = ATTN_BLOCK
    n_pairs = d // LANES
    return pl.pallas_call(
        _attn_kernel,
        grid=(batch,),
        in_specs=[
            pl.BlockSpec((seq, d), lambda b: (b, 0)),
            pl.BlockSpec((seq, d), lambda b: (b, 0)),
            pl.BlockSpec((seq, d), lambda b: (b, 1)),
        ],
        out_specs=pl.BlockSpec((seq, d), lambda b: (b, 0)),
        out_shape=jax.ShapeDtypeStruct((m, d), _BF16),
        scratch_shapes=[
            pltpu.VMEM((n_pairs, 2 * blk, LANES), _BF16),
            pltpu.VMEM((n_pairs, SUBLANES, 2 * blk), _F32),
            pltpu.VMEM((2, n_pairs, LANES, 2 * blk), _F32),
            pltpu.SMEM((1,), _F32),
        ],
        compiler_params=_params(1),
        name="stick_breaking",
    )(q, kv, kv)


def kernel(x, p, ln_mix_a, w_in_a, g_v_a, w_spatial, b_spatial, w_out_a, ln_kv, w_kv, g_k,
           ln_mix_b, w_q, g_q, w_out_b, ln_mlp, w_up, w_down, ln_ple, w_ple_gate, w_ple_proj):
    batch, seq, d = x.shape
    depth = p.shape[0]
    n_a = ln_mix_a.shape[0]
    m = batch * seq
    xf = x.reshape(m, d)
    pf = p.reshape(depth, m, p.shape[-1])
    rows = lambda g: g.reshape(g.shape[0], 1, g.shape[1])
    pairs = lambda g: jnp.concatenate([g, g], axis=-1).reshape(g.shape[0], 1, LANES)
    ln_mix_a, g_v_a, ln_mix_b, ln_mlp, ln_ple = map(rows, (ln_mix_a, g_v_a, ln_mix_b, ln_mlp, ln_ple))
    w_ple_proj = w_ple_proj.astype(_BF16)
    bias = jnp.repeat(jnp.swapaxes(b_spatial, 1, 2), d // b_spatial.shape[1], axis=2)

    def post_weights(i):
        out = (w_out_a, i) if i < n_a else (w_out_b, i - n_a)
        return {("out", i): out, ("up", i): (w_up, i), ("down", i): (w_down, i),
                ("gate", i): (w_ple_gate, i)}

    cast = {}

    def bf16_weight(name, source):
        if name in cast:
            return cast.pop(name)[None], 0
        stacked, layer = source
        return stacked.astype(_BF16), layer

    def ride(call, wanted):
        names = list(wanted)
        main, *casts = call([wanted[n] for n in names])
        cast.update(zip(names, casts))
        return main

    def q_group(j):
        return dict(params=[(ln_mix_b, j), bf16_weight(("q", j), (w_q, j)), (pairs(g_q), j)],
                    n_norm=d // LANES, out_scale=-LOG2_E * HEAD_DIM ** -0.5, segment_major=False)

    kv_shared = q = None
    for i in range(depth):
        if i < n_a:
            mixer_params = [(ln_mix_a, i), (w_in_a, i), (g_v_a, i), (w_spatial, i), (bias, i)]
            wanted = post_weights(i) if i == 0 else {}
            y = ride(lambda riders: _sgu_call(xf, mixer_params, riders), wanted)
        else:
            if q is None:
                q, = _proj_call(xf, [q_group(i - n_a)])
            y = _attn_call(q, kv_shared, batch, seq)
        weights = {n: bf16_weight(n, src) for n, src in post_weights(i).items()}
        post_params = [weights[("out", i)], (ln_mlp, i), weights[("up", i)], weights[("down", i)],
                       (ln_ple, i), weights[("gate", i)], (w_ple_proj, i)]
        wanted = post_weights(i + 1) if i + 1 < depth else {}
        if i == n_a - 1:
            wanted[("kv", 0)] = (w_kv[None], 0)
        if n_a <= i + 1 < depth:
            wanted[("q", i + 1 - n_a)] = (w_q, i + 1 - n_a)
        xf = ride(lambda riders: _post_call(xf, y, (pf, i), post_params, riders), wanted)
        groups = []
        if n_a <= i + 1 < depth:
            groups.append(q_group(i + 1 - n_a))
        if i == n_a - 1:
            groups.append(dict(
                params=[(rows(ln_kv[None]), 0), bf16_weight(("kv", 0), (w_kv[None], 0)),
                        (pairs(g_k[None]), 0)],
                n_norm=d // LANES, out_scale=1.0, segment_major=True))
        projected = list(_proj_call(xf, groups)) if groups else []
        q = projected.pop(0) if n_a <= i + 1 < depth else None
        if i == n_a - 1:
            kv_shared = projected.pop(0)
    return xf.reshape(batch, seq, d)
```

```python
import functools

import jax
import jax.numpy as jnp
from jax import lax
from jax.experimental import pallas as pl
from jax.experimental.pallas import tpu as pltpu

EPS = 1e-6
CHUNK = 128
HEAD_DIM = 64
LANES = 128
MXU_WIDTH = 256
BF16_ROWS = 16
TOKEN_TILE = 512
WIDE_TOKEN_TILE = 1024
ATTN_BLOCK = 128
STATIC_SWEEP = 3
assert STATIC_SWEEP >= 2
FF_CHUNK = 1024
VMEM_LIMIT = 56 * 1024 * 1024
SUBLANES = 8
KEY_SEGMENT = ATTN_BLOCK // SUBLANES
LOG2_E = 1.4426950408889634
WEIGHT_FLOOR = 2.0 ** -126

_BF16 = jnp.bfloat16
_F32 = jnp.float32


def _rms(x, g):
    return x * lax.rsqrt(jnp.mean(x * x, axis=-1, keepdims=True) + EPS) * g


def _gelu_tanh(x):
    c = -2.0 * LOG2_E * (2.0 / jnp.pi) ** 0.5
    return x / (1.0 + jnp.exp2(x * (x * x * (c * 0.044715) + c)))


def _layer_spec(param):
    stacked, layer = param
    rest = stacked.shape[1:]
    return pl.BlockSpec((None,) + rest, lambda *_: (layer,) + (0,) * len(rest),
                        pipeline_mode=pl.Buffered(1))


def _rider_specs(riders, n_steps):
    arrays, in_specs, out_specs, out_shapes = [], [], [], []
    for stacked, layer in riders:
        _, r, c = stacked.shape
        rb = r // n_steps
        assert rb * n_steps == r and rb % BF16_ROWS == 0, (stacked.shape, n_steps)
        arrays.append(stacked)
        in_specs.append(pl.BlockSpec((None, rb, c), lambda i, layer=layer: (layer, i, 0)))
        out_specs.append(pl.BlockSpec((rb, c), lambda i: (i, 0)))
        out_shapes.append(jax.ShapeDtypeStruct((r, c), _BF16))
    return arrays, in_specs, out_specs, out_shapes


def _cast_riders(src_refs, dst_refs):
    for src, dst in zip(src_refs, dst_refs):
        dst[...] = src[...].astype(dst.dtype)


def _params(n_axes):
    return pltpu.CompilerParams(
        dimension_semantics=("arbitrary",) * n_axes, vmem_limit_bytes=VMEM_LIMIT)


def _sgu_kernel(x_ref, ln_ref, win_ref, gv_ref, ws_ref, bias_ref, *rest, n_riders):
    y_ref = rest[n_riders]
    _cast_riders(rest[:n_riders], rest[n_riders + 1:])
    tm, d = x_ref.shape
    n_groups = ws_ref.shape[0]
    n_chunks = tm // CHUNK
    h = _rms(x_ref[...], ln_ref[...]).astype(_BF16)
    v = _gelu_tanh(jnp.dot(h, win_ref[:, d:].astype(_BF16), preferred_element_type=_F32))
    v = _rms(v, gv_ref[...]).astype(_BF16)
    u_cols = d // MXU_WIDTH
    u_tiles = [_gelu_tanh(jnp.dot(h, win_ref[:, c * MXU_WIDTH:(c + 1) * MXU_WIDTH].astype(_BF16),
                                  preferred_element_type=_F32)) for c in range(u_cols - 1)]
    row = lax.broadcasted_iota(jnp.int32, (CHUNK, CHUNK), 0)
    col = lax.broadcasted_iota(jnp.int32, (CHUNK, CHUNK), 1)
    causal = col <= row
    mix_cols = []
    for g in range(n_groups):
        w = jnp.where(causal, ws_ref[g], 0.0).astype(_BF16)
        vg = jnp.concatenate(
            [v[c * CHUNK:(c + 1) * CHUNK, g * LANES:(g + 1) * LANES] for c in range(n_chunks)],
            axis=1)
        mg = jnp.dot(w, vg, preferred_element_type=_F32)
        mix_cols.append(jnp.concatenate(
            [mg[:, c * LANES:(c + 1) * LANES] for c in range(n_chunks)], axis=0))
    mix = jnp.concatenate(mix_cols, axis=1)
    bias = jnp.concatenate([bias_ref[...]] * n_chunks, axis=0)
    u_tiles.append(_gelu_tanh(jnp.dot(h, win_ref[:, d - MXU_WIDTH:d].astype(_BF16),
                                      preferred_element_type=_F32)))
    for c, u in enumerate(u_tiles):
        cols = slice(c * MXU_WIDTH, (c + 1) * MXU_WIDTH)
        y_ref[:, cols] = (u * (mix[:, cols] + bias[:, cols])).astype(y_ref.dtype)


def _sgu_call(x, params, riders=()):
    m, d = x.shape
    tm = WIDE_TOKEN_TILE
    tile = pl.BlockSpec((tm, d), lambda i: (i, 0))
    r_arrays, r_in, r_out, r_shapes = _rider_specs(riders, m // tm)
    return pl.pallas_call(
        functools.partial(_sgu_kernel, n_riders=len(riders)),
        grid=(m // tm,),
        in_specs=[tile] + [_layer_spec(a) for a in params] + r_in,
        out_specs=[tile] + r_out,
        out_shape=[jax.ShapeDtypeStruct((m, d), _BF16)] + r_shapes,
        compiler_params=_params(1),
        name="sgu_mixer",
    )(x, *[a for a, _ in params], *r_arrays)


def _post_kernel(x_ref, y_ref, p_ref, wo_ref, lnm_ref, wup_ref, wdn_ref,
                 lnp_ref, wg_ref, wp_ref, *rest, n_riders):
    o_ref = rest[n_riders]
    _cast_riders(rest[:n_riders], rest[n_riders + 1:])
    d_ff = wup_ref.shape[1]
    x = x_ref[...] + jnp.dot(y_ref[...], wo_ref[...], preferred_element_type=_F32)
    h = _rms(x, lnm_ref[...]).astype(_BF16)
    acc = x
    for c in range(d_ff // FF_CHUNK):
        a = jnp.dot(h, wup_ref[:, c * FF_CHUNK:(c + 1) * FF_CHUNK], preferred_element_type=_F32)
        a = jnp.maximum(a, 0.0)
        a = (a * a).astype(_BF16)
        acc = acc + jnp.dot(a, wdn_ref[c * FF_CHUNK:(c + 1) * FF_CHUNK, :],
                            preferred_element_type=_F32)
    x = acc
    hg = _rms(x, lnp_ref[...]).astype(_BF16)
    gate = 1.0 / (1.0 + jnp.exp(-jnp.dot(hg, wg_ref[...], preferred_element_type=_F32)))
    pp = jnp.dot(p_ref[...].astype(_BF16), wp_ref[...], preferred_element_type=_F32)
    o_ref[...] = x + pp * gate


def _post_call(x, y, p_param, params, riders=()):
    m, d = x.shape
    tm = TOKEN_TILE if riders else WIDE_TOKEN_TILE
    tile = pl.BlockSpec((tm, d), lambda i: (i, 0))
    p, p_layer = p_param
    r_arrays, r_in, r_out, r_shapes = _rider_specs(riders, m // tm)
    return pl.pallas_call(
        functools.partial(_post_kernel, n_riders=len(riders)),
        grid=(m // tm,),
        in_specs=[tile, tile, pl.BlockSpec((None, tm, p.shape[2]), lambda i: (p_layer, i, 0))]
        + [_layer_spec(a) for a in params] + r_in,
        out_specs=[tile] + r_out,
        out_shape=[jax.ShapeDtypeStruct((m, d), _F32)] + r_shapes,
        compiler_params=_params(1),
        name="post_mlp_ple",
    )(x, y, p, *[a for a, _ in params], *r_arrays)


def _proj_kernel(x_ref, *refs, groups, n_riders):
    n_in = 3 * len(groups)
    outs = refs[n_in + n_riders:]
    _cast_riders(refs[n_in:n_in + n_riders], outs[len(groups):])
    x = x_ref[...]
    for n, cfg in enumerate(groups):
        _project_heads(x, *refs[3 * n:3 * n + 3], outs[n], **cfg)


def _project_heads(x, ln_ref, w_ref, g_ref, o_ref, *, n_norm, out_scale, segment_major):
    tm = x.shape[0]
    h = _rms(x, ln_ref[...]).astype(_BF16)
    if segment_major:
        r = lax.broadcasted_iota(jnp.int32, (ATTN_BLOCK, ATTN_BLOCK), 0)
        c = lax.broadcasted_iota(jnp.int32, (ATTN_BLOCK, ATTN_BLOCK), 1)
        src = KEY_SEGMENT * (r & (SUBLANES - 1)) + r // SUBLANES
        perm = jnp.where(c == src, 1.0, 0.0).astype(_BF16)
        h = jnp.concatenate(
            [jnp.dot(perm, h[b * ATTN_BLOCK:(b + 1) * ATTN_BLOCK],
                     preferred_element_type=_F32).astype(_BF16)
             for b in range(tm // ATTN_BLOCK)], axis=0)
    first_head = lax.broadcasted_iota(jnp.int32, (tm, LANES), 1) < HEAD_DIM
    gain = g_ref[...] * out_scale
    for c in range(w_ref.shape[1] // MXU_WIDTH):
        y = jnp.dot(h, w_ref[:, c * MXU_WIDTH:(c + 1) * MXU_WIDTH], preferred_element_type=_F32)
        for t in range(MXU_WIDTH // LANES):
            blk = y[:, t * LANES:(t + 1) * LANES]
            col = c * MXU_WIDTH + t * LANES
            if col < n_norm * LANES:
                sq = blk * blk
                s_a = jnp.sum(jnp.where(first_head, sq, 0.0), axis=-1, keepdims=True)
                s_b = jnp.sum(jnp.where(first_head, 0.0, sq), axis=-1, keepdims=True)
                ms = jnp.where(first_head, s_a, s_b) * (1.0 / HEAD_DIM)
                blk = blk * (lax.rsqrt(ms + EPS) * gain)
            o_ref[:, col:col + LANES] = blk.astype(o_ref.dtype)


def _proj_call(x, groups, riders=()):
    m, d = x.shape
    tm = WIDE_TOKEN_TILE
    tile = lambda width: pl.BlockSpec((tm, width), lambda i: (i, 0))
    params = [a for g in groups for a in g["params"]]
    widths = [g["params"][1][0].shape[2] for g in groups]
    cfgs = tuple({k: g[k] for k in ("n_norm", "out_scale", "segment_major")} for g in groups)
    r_arrays, r_in, r_out, r_shapes = _rider_specs(riders, m // tm)
    return pl.pallas_call(
        functools.partial(_proj_kernel, groups=cfgs, n_riders=len(riders)),
        grid=(m // tm,),
        in_specs=[tile(d)] + [_layer_spec(a) for a in params] + r_in,
        out_specs=[tile(n) for n in widths] + r_out,
        out_shape=[jax.ShapeDtypeStruct((m, n), _BF16) for n in widths] + r_shapes,
        compiler_params=_params(1),
        name="head_proj",
    )(x, *[a for a, _ in params], *r_arrays)


def _attn_kernel(q_ref, k_ref, v_ref, o_ref, qs_ref, passed_ref, acc_ref, bound_ref):
    blk = ATTN_BLOCK
    n_pairs = qs_ref.shape[0]
    n_groups = blk // SUBLANES
    lane = lax.broadcasted_iota(jnp.int32, (blk, LANES), 1)
    first_head = lane < HEAD_DIM
    sub = lax.broadcasted_iota(jnp.int32, (SUBLANES, 2 * blk), 0)

    def block_weights(z, passed, valid):
        beta = 1.0 / (1.0 + jnp.exp2(z))
        keep = 1.0 - beta
        if valid is not None:
            keep = jnp.where(valid, keep, 1.0)
        after = [None] * n_groups
        run = None
        for v in range(n_groups - 1, -1, -1):
            after[v] = run
            kv = keep[v * SUBLANES:(v + 1) * SUBLANES]
            run = kv if run is None else run * kv
        later = jnp.where(sub + 1 < SUBLANES, pltpu.roll(run, SUBLANES - 1, 0), 1.0)
        for step in (1, 2, 4):
            shifted = pltpu.roll(later, SUBLANES - step, 0)
            later = later * jnp.where(sub + step < SUBLANES, shifted, 1.0)
        whole = later * run
        base = later if passed is None else later * passed
        parts = []
        for v in range(n_groups):
            scale = base if after[v] is None else after[v] * base
            parts.append(beta[v * SUBLANES:(v + 1) * SUBLANES] * scale)
        w = jnp.concatenate(parts, axis=0)
        if valid is not None:
            w = jnp.where(valid, w, 0.0)
        total = jnp.broadcast_to(whole[0:1], whole.shape)
        return w.astype(_BF16), total if passed is None else passed * total

    def sweep(j_last, n_blocks, first, slot):
        rows = pl.ds(pl.multiple_of((j_last - (n_blocks - 1)) * blk, blk), n_blocks * blk)
        valid = None
        if first:
            r = lax.broadcasted_iota(jnp.int32, (blk, 2 * blk), 0)
            key = KEY_SEGMENT * (r & (SUBLANES - 1)) + r // SUBLANES
            query = lax.broadcasted_iota(jnp.int32, (blk, 2 * blk), 1) & (blk - 1)
            valid = key < query
        pairs = range(n_pairs)
        zs = [lax.dot_general(k_ref[rows, p * LANES:(p + 1) * LANES], qs_ref[p],
                              (((1,), (1,)), ((), ())), preferred_element_type=_F32)
              for p in pairs]
        weights = []
        top = None
        for p in pairs:
            passed = None if first else passed_ref[p]
            parts = [None] * n_blocks
            for b in range(n_blocks - 1, -1, -1):
                parts[b], passed = block_weights(
                    zs[p][b * blk:(b + 1) * blk], passed,
                    valid if first and b == n_blocks - 1 else None)
            passed_ref[p] = passed
            top = passed if top is None else jnp.maximum(top, passed)
            weights.append(jnp.concatenate(parts, axis=0) if n_blocks > 1 else parts[0])
        bound_ref[0] = jnp.max(top)
        for p in pairs:
            acc = lax.dot_general(v_ref[rows, p * LANES:(p + 1) * LANES], weights[p],
                                  (((0,), (0,)), ((), ())), preferred_element_type=_F32)
            acc_ref[slot, p] = acc if first else acc_ref[slot, p] + acc

    def write_out(i, slot):
        rows = pl.ds(pl.multiple_of(i * blk, blk), blk)
        for p in range(n_pairs):
            acc = acc_ref[slot, p]
            out_t = jnp.concatenate([acc[:HEAD_DIM, :blk], acc[HEAD_DIM:, blk:]], axis=0)
            o_ref[rows, p * LANES:(p + 1) * LANES] = out_t.T.astype(o_ref.dtype)

    def query_block(i, carry):
        rows = pl.ds(pl.multiple_of(i * blk, blk), blk)
        slot = i & 1
        for p in range(n_pairs):
            q2 = q_ref[rows, p * LANES:(p + 1) * LANES]
            zeros = jnp.zeros_like(q2)
            qs_ref[p] = jnp.concatenate(
                [jnp.where(first_head, q2, zeros), jnp.where(first_head, zeros, q2)], axis=0)

        for n in range(1, STATIC_SWEEP + 1):
            @pl.when((i == n - 1) if n < STATIC_SWEEP else (i >= n - 1))
            def _(n=n):
                if n > 1:
                    write_out(i - 1, 1 - slot)
                sweep(i, n, first=True, slot=slot)

        def cond(state):
            j, top = state
            return jnp.logical_and(j >= 0, top >= WEIGHT_FLOOR)

        def body(state):
            j, _ = state
            sweep(j, 1, first=False, slot=slot)
            return j - 1, bound_ref[0]

        lax.while_loop(cond, body, (i - STATIC_SWEEP, bound_ref[0]))
        return carry

    n_q = q_ref.shape[0] // blk
    lax.fori_loop(0, n_q, query_block, 0)
    write_out(n_q - 1, (n_q - 1) & 1)


def _attn_call(q, kv, batch, seq):
    m, d = q.shape
    blk = ATTN_BLOCK
    n_pairs = d // LANES
    return pl.pallas_call(
        _attn_kernel,
        grid=(batch,),
        in_specs=[
            pl.BlockSpec((seq, d), lambda b: (b, 0)),
            pl.BlockSpec((seq, d), lambda b: (b, 0)),
            pl.BlockSpec((seq, d), lambda b: (b, 1)),
        ],
        out_specs=pl.BlockSpec((seq, d), lambda b: (b, 0)),
        out_shape=jax.ShapeDtypeStruct((m, d), _BF16),
        scratch_shapes=[
            pltpu.VMEM((n_pairs, 2 * blk, LANES), _BF16),
            pltpu.VMEM((n_pairs, SUBLANES, 2 * blk), _F32),
            pltpu.VMEM((2, n_pairs, LANES, 2 * blk), _F32),
            pltpu.SMEM((1,), _F32),
        ],
        compiler_params=_params(1),
        name="stick_breaking",
    )(q, kv, kv)


def kernel(x, p, ln_mix_a, w_in_a, g_v_a, w_spatial, b_spatial, w_out_a, ln_kv, w_kv, g_k,
           ln_mix_b, w_q, g_q, w_out_b, ln_mlp, w_up, w_down, ln_ple, w_ple_gate, w_ple_proj):
    batch, seq, d = x.shape
    depth = p.shape[0]
    n_a = ln_mix_a.shape[0]
    m = batch * seq
    xf = x.reshape(m, d)
    pf = p.reshape(depth, m, p.shape[-1])
    rows = lambda g: g.reshape(g.shape[0], 1, g.shape[1])
    pairs = lambda g: jnp.concatenate([g, g], axis=-1).reshape(g.shape[0], 1, LANES)
    ln_mix_a, g_v_a, ln_mix_b, ln_mlp, ln_ple = map(rows, (ln_mix_a, g_v_a, ln_mix_b, ln_mlp, ln_ple))
    w_ple_proj = w_ple_proj.astype(_BF16)
    bias = jnp.repeat(jnp.swapaxes(b_spatial, 1, 2), d // b_spatial.shape[1], axis=2)

    def post_weights(i):
        out = (w_out_a, i) if i < n_a else (w_out_b, i - n_a)
        return {("out", i): out, ("up", i): (w_up, i), ("down", i): (w_down, i),
                ("gate", i): (w_ple_gate, i)}

    cast = {}

    def bf16_weight(name, source):
        if name in cast:
            return cast.pop(name)[None], 0
        stacked, layer = source
        return stacked.astype(_BF16), layer

    def ride(call, wanted):
        names = list(wanted)
        main, *casts = call([wanted[n] for n in names])
        cast.update(zip(names, casts))
        return main

    def q_group(j):
        return dict(params=[(ln_mix_b, j), bf16_weight(("q", j), (w_q, j)), (pairs(g_q), j)],
                    n_norm=d // LANES, out_scale=-LOG2_E * HEAD_DIM ** -0.5, segment_major=False)

    kv_shared = q = None
    for i in range(depth):
        projects_kv = i == n_a - 1
        projects_q = n_a <= i + 1 < depth
        proj_weights = {}
        if projects_kv:
            proj_weights[("kv", 0)] = (w_kv[None], 0)
        if projects_q:
            proj_weights[("q", i + 1 - n_a)] = (w_q, i + 1 - n_a)
        if i < n_a:
            mixer_params = [(ln_mix_a, i), (w_in_a, i), (g_v_a, i), (w_spatial, i), (bias, i)]
            wanted = {**(post_weights(i) if i == 0 else {}), **proj_weights}
            proj_weights = {}
            y = ride(lambda riders: _sgu_call(xf, mixer_params, riders), wanted)
        else:
            if q is None:
                q, = _proj_call(xf, [q_group(i - n_a)])
            y = _attn_call(q, kv_shared, batch, seq)
        weights = {n: bf16_weight(n, src) for n, src in post_weights(i).items()}
        post_params = [weights[("out", i)], (ln_mlp, i), weights[("up", i)], weights[("down", i)],
                       (ln_ple, i), weights[("gate", i)], (w_ple_proj, i)]
        next_weights = post_weights(i + 1) if i + 1 < depth else {}
        has_proj = projects_kv or projects_q
        wanted = {**proj_weights, **({} if has_proj else next_weights)}
        xf = ride(lambda riders: _post_call(xf, y, (pf, i), post_params, riders), wanted)
        groups = []
        if projects_q:
            groups.append(q_group(i + 1 - n_a))
        if projects_kv:
            groups.append(dict(
                params=[(rows(ln_kv[None]), 0), bf16_weight(("kv", 0), (w_kv[None], 0)),
                        (pairs(g_k[None]), 0)],
                n_norm=d // LANES, out_scale=1.0, segment_major=True))
        if has_proj:
            names = list(next_weights)
            outs = list(_proj_call(xf, groups, [next_weights[n] for n in names]))
            cast.update(zip(names, outs[len(groups):]))
            q = outs.pop(0) if projects_q else None
            if projects_kv:
                kv_shared = outs.pop(0)
        else:
            q = None
    return xf.reshape(batch, seq, d)
```

```python
import functools

import jax
import jax.numpy as jnp
from jax import lax
from jax.experimental import pallas as pl
from jax.experimental.pallas import tpu as pltpu

EPS = 1e-6
CHUNK = 128
HEAD_DIM = 64
LANES = 128
MXU_WIDTH = 256
BF16_ROWS = 16
TOKEN_TILE = 512
WIDE_TOKEN_TILE = 1024
ATTN_BLOCK = 128
STATIC_SWEEP = 3
assert STATIC_SWEEP >= 2
FF_CHUNK = 1024
VMEM_LIMIT = 56 * 1024 * 1024
SUBLANES = 8
KEY_SEGMENT = ATTN_BLOCK // SUBLANES
LOG2_E = 1.4426950408889634
WEIGHT_FLOOR = 2.0 ** -126

_BF16 = jnp.bfloat16
_F32 = jnp.float32


def _rms(x, g):
    return x * lax.rsqrt(jnp.mean(x * x, axis=-1, keepdims=True) + EPS) * g


def _gelu_tanh(x):
    c = -2.0 * LOG2_E * (2.0 / jnp.pi) ** 0.5
    return x / (1.0 + jnp.exp2(x * (x * x * (c * 0.044715) + c)))


def _layer_spec(param):
    stacked, layer = param
    rest = stacked.shape[1:]
    return pl.BlockSpec((None,) + rest, lambda *_: (layer,) + (0,) * len(rest),
                        pipeline_mode=pl.Buffered(1))


def _rider_specs(riders, n_steps):
    arrays, in_specs, out_specs, out_shapes = [], [], [], []
    for stacked, layer in riders:
        _, r, c = stacked.shape
        rb = r // n_steps
        assert rb * n_steps == r and rb % BF16_ROWS == 0, (stacked.shape, n_steps)
        arrays.append(stacked)
        in_specs.append(pl.BlockSpec((None, rb, c), lambda i, layer=layer: (layer, i, 0)))
        out_specs.append(pl.BlockSpec((rb, c), lambda i: (i, 0)))
        out_shapes.append(jax.ShapeDtypeStruct((r, c), _BF16))
    return arrays, in_specs, out_specs, out_shapes


def _cast_riders(src_refs, dst_refs):
    for src, dst in zip(src_refs, dst_refs):
        dst[...] = src[...].astype(dst.dtype)


def _params(n_axes):
    return pltpu.CompilerParams(
        dimension_semantics=("arbitrary",) * n_axes, vmem_limit_bytes=VMEM_LIMIT)


def _sgu_kernel(x_ref, ln_ref, win_ref, gv_ref, ws_ref, bias_ref, *rest, n_riders):
    y_ref = rest[n_riders]
    _cast_riders(rest[:n_riders], rest[n_riders + 1:])
    tm, d = x_ref.shape
    n_groups = ws_ref.shape[0]
    n_chunks = tm // CHUNK
    h = _rms(x_ref[...], ln_ref[...]).astype(_BF16)
    v = _gelu_tanh(jnp.dot(h, win_ref[:, d:].astype(_BF16), preferred_element_type=_F32))
    v = _rms(v, gv_ref[...]).astype(_BF16)
    u_cols = d // MXU_WIDTH
    u_tiles = [_gelu_tanh(jnp.dot(h, win_ref[:, c * MXU_WIDTH:(c + 1) * MXU_WIDTH].astype(_BF16),
                                  preferred_element_type=_F32)) for c in range(u_cols - 1)]
    row = lax.broadcasted_iota(jnp.int32, (CHUNK, CHUNK), 0)
    col = lax.broadcasted_iota(jnp.int32, (CHUNK, CHUNK), 1)
    causal = col <= row
    mix_cols = []
    for g in range(n_groups):
        w = jnp.where(causal, ws_ref[g], 0.0).astype(_BF16)
        vg = jnp.concatenate(
            [v[c * CHUNK:(c + 1) * CHUNK, g * LANES:(g + 1) * LANES] for c in range(n_chunks)],
            axis=1)
        mg = jnp.dot(w, vg, preferred_element_type=_F32)
        mix_cols.append(jnp.concatenate(
            [mg[:, c * LANES:(c + 1) * LANES] for c in range(n_chunks)], axis=0))
    mix = jnp.concatenate(mix_cols, axis=1)
    bias = jnp.concatenate([bias_ref[...]] * n_chunks, axis=0)
    u_tiles.append(_gelu_tanh(jnp.dot(h, win_ref[:, d - MXU_WIDTH:d].astype(_BF16),
                                      preferred_element_type=_F32)))
    for c, u in enumerate(u_tiles):
        cols = slice(c * MXU_WIDTH, (c + 1) * MXU_WIDTH)
        y_ref[:, cols] = (u * (mix[:, cols] + bias[:, cols])).astype(y_ref.dtype)


def _sgu_call(x, params, riders=()):
    m, d = x.shape
    tm = WIDE_TOKEN_TILE
    tile = pl.BlockSpec((tm, d), lambda i: (i, 0))
    r_arrays, r_in, r_out, r_shapes = _rider_specs(riders, m // tm)
    return pl.pallas_call(
        functools.partial(_sgu_kernel, n_riders=len(riders)),
        grid=(m // tm,),
        in_specs=[tile] + [_layer_spec(a) for a in params] + r_in,
        out_specs=[tile] + r_out,
        out_shape=[jax.ShapeDtypeStruct((m, d), _BF16)] + r_shapes,
        compiler_params=_params(1),
        name="sgu_mixer",
    )(x, *[a for a, _ in params], *r_arrays)


def _post_kernel(x_ref, y_ref, p_ref, wo_ref, lnm_ref, wup_ref, wdn_ref,
                 lnp_ref, wg_ref, wp_ref, *rest, n_riders):
    o_ref = rest[n_riders]
    _cast_riders(rest[:n_riders], rest[n_riders + 1:])
    d_ff = wup_ref.shape[1]
    x = x_ref[...] + jnp.dot(y_ref[...], wo_ref[...], preferred_element_type=_F32)
    h = _rms(x, lnm_ref[...]).astype(_BF16)
    acc = x
    for c in range(d_ff // FF_CHUNK):
        a = jnp.dot(h, wup_ref[:, c * FF_CHUNK:(c + 1) * FF_CHUNK], preferred_element_type=_F32)
        a = jnp.maximum(a, 0.0)
        a = (a * a).astype(_BF16)
        acc = acc + jnp.dot(a, wdn_ref[c * FF_CHUNK:(c + 1) * FF_CHUNK, :],
                            preferred_element_type=_F32)
    x = acc
    hg = _rms(x, lnp_ref[...]).astype(_BF16)
    gate = 1.0 / (1.0 + jnp.exp(-jnp.dot(hg, wg_ref[...], preferred_element_type=_F32)))
    pp = jnp.dot(p_ref[...].astype(_BF16), wp_ref[...], preferred_element_type=_F32)
    o_ref[...] = x + pp * gate


def _post_call(x, y, p_param, params, riders=()):
    m, d = x.shape
    tm = TOKEN_TILE if riders else WIDE_TOKEN_TILE
    tile = pl.BlockSpec((tm, d), lambda i: (i, 0))
    p, p_layer = p_param
    r_arrays, r_in, r_out, r_shapes = _rider_specs(riders, m // tm)
    return pl.pallas_call(
        functools.partial(_post_kernel, n_riders=len(riders)),
        grid=(m // tm,),
        in_specs=[tile, tile, pl.BlockSpec((None, tm, p.shape[2]), lambda i: (p_layer, i, 0))]
        + [_layer_spec(a) for a in params] + r_in,
        out_specs=[tile] + r_out,
        out_shape=[jax.ShapeDtypeStruct((m, d), _F32)] + r_shapes,
        compiler_params=_params(1),
        name="post_mlp_ple",
    )(x, y, p, *[a for a, _ in params], *r_arrays)


def _proj_kernel(x_ref, *refs, groups, n_riders):
    n_in = 3 * len(groups)
    outs = refs[n_in + n_riders:]
    _cast_riders(refs[n_in:n_in + n_riders], outs[len(groups):])
    x = x_ref[...]
    for n, cfg in enumerate(groups):
        _project_heads(x, *refs[3 * n:3 * n + 3], outs[n], **cfg)


def _project_heads(x, ln_ref, w_ref, g_ref, o_ref, *, n_norm, out_scale, segment_major):
    tm = x.shape[0]
    h = _rms(x, ln_ref[...]).astype(_BF16)
    if segment_major:
        r = lax.broadcasted_iota(jnp.int32, (ATTN_BLOCK, ATTN_BLOCK), 0)
        c = lax.broadcasted_iota(jnp.int32, (ATTN_BLOCK, ATTN_BLOCK), 1)
        src = KEY_SEGMENT * (r & (SUBLANES - 1)) + r // SUBLANES
        perm = jnp.where(c == src, 1.0, 0.0).astype(_BF16)
        h = jnp.concatenate(
            [jnp.dot(perm, h[b * ATTN_BLOCK:(b + 1) * ATTN_BLOCK],
                     preferred_element_type=_F32).astype(_BF16)
             for b in range(tm // ATTN_BLOCK)], axis=0)
    first_head = lax.broadcasted_iota(jnp.int32, (tm, LANES), 1) < HEAD_DIM
    gain = g_ref[...] * out_scale
    for c in range(w_ref.shape[1] // MXU_WIDTH):
        y = jnp.dot(h, w_ref[:, c * MXU_WIDTH:(c + 1) * MXU_WIDTH], preferred_element_type=_F32)
        for t in range(MXU_WIDTH // LANES):
            blk = y[:, t * LANES:(t + 1) * LANES]
            col = c * MXU_WIDTH + t * LANES
            if col < n_norm * LANES:
                sq = blk * blk
                s_a = jnp.sum(jnp.where(first_head, sq, 0.0), axis=-1, keepdims=True)
                s_b = jnp.sum(jnp.where(first_head, 0.0, sq), axis=-1, keepdims=True)
                ms = jnp.where(first_head, s_a, s_b) * (1.0 / HEAD_DIM)
                blk = blk * (lax.rsqrt(ms + EPS) * gain)
            o_ref[:, col:col + LANES] = blk.astype(o_ref.dtype)


def _proj_call(x, groups, riders=()):
    m, d = x.shape
    tm = WIDE_TOKEN_TILE
    tile = lambda width: pl.BlockSpec((tm, width), lambda i: (i, 0))
    params = [a for g in groups for a in g["params"]]
    widths = [g["params"][1][0].shape[2] for g in groups]
    cfgs = tuple({k: g[k] for k in ("n_norm", "out_scale", "segment_major")} for g in groups)
    r_arrays, r_in, r_out, r_shapes = _rider_specs(riders, m // tm)
    return pl.pallas_call(
        functools.partial(_proj_kernel, groups=cfgs, n_riders=len(riders)),
        grid=(m // tm,),
        in_specs=[tile(d)] + [_layer_spec(a) for a in params] + r_in,
        out_specs=[tile(n) for n in widths] + r_out,
        out_shape=[jax.ShapeDtypeStruct((m, n), _BF16) for n in widths] + r_shapes,
        compiler_params=_params(1),
        name="head_proj",
    )(x, *[a for a, _ in params], *r_arrays)


def _attn_kernel(q_ref, k_ref, v_ref, o_ref, qs_ref, passed_ref, acc_ref, bound_ref):
    blk = ATTN_BLOCK
    n_pairs = qs_ref.shape[0]
    n_groups = blk // SUBLANES
    lane = lax.broadcasted_iota(jnp.int32, (blk, LANES), 1)
    first_head = lane < HEAD_DIM
    sub = lax.broadcasted_iota(jnp.int32, (SUBLANES, 2 * blk), 0)

    def block_weights(z, passed, valid):
        beta = 1.0 / (1.0 + jnp.exp2(z))
        keep = 1.0 - beta
        if valid is not None:
            keep = jnp.where(valid, keep, 1.0)
        after = [None] * n_groups
        run = None
        for v in range(n_groups - 1, -1, -1):
            after[v] = run
            kv = keep[v * SUBLANES:(v + 1) * SUBLANES]
            run = kv if run is None else run * kv
        later = jnp.where(sub + 1 < SUBLANES, pltpu.roll(run, SUBLANES - 1, 0), 1.0)
        for step in (1, 2, 4):
            shifted = pltpu.roll(later, SUBLANES - step, 0)
            later = later * jnp.where(sub + step < SUBLANES, shifted, 1.0)
        whole = later * run
        base = later if passed is None else later * passed
        parts = []
        for v in range(n_groups):
            scale = base if after[v] is None else after[v] * base
            parts.append(beta[v * SUBLANES:(v + 1) * SUBLANES] * scale)
        w = jnp.concatenate(parts, axis=0)
        if valid is not None:
            w = jnp.where(valid, w, 0.0)
        total = jnp.broadcast_to(whole[0:1], whole.shape)
        return w.astype(_BF16), total if passed is None else passed * total

    def sweep(j_last, n_blocks, first, slot):
        rows = pl.ds(pl.multiple_of((j_last - (n_blocks - 1)) * blk, blk), n_blocks * blk)
        valid = None
        if first:
            r = lax.broadcasted_iota(jnp.int32, (blk, 2 * blk), 0)
            key = KEY_SEGMENT * (r & (SUBLANES - 1)) + r // SUBLANES
            query = lax.broadcasted_iota(jnp.int32, (blk, 2 * blk), 1) & (blk - 1)
            valid = key < query
        pairs = range(n_pairs)
        zs = [lax.dot_general(k_ref[rows, p * LANES:(p + 1) * LANES], qs_ref[p],
                              (((1,), (1,)), ((), ())), preferred_element_type=_F32)
              for p in pairs]
        weights = []
        top = None
        for p in pairs:
            passed = None if first else passed_ref[p]
            parts = [None] * n_blocks
            for b in range(n_blocks - 1, -1, -1):
                parts[b], passed = block_weights(
                    zs[p][b * blk:(b + 1) * blk], passed,
                    valid if first and b == n_blocks - 1 else None)
            passed_ref[p] = passed
            top = passed if top is None else jnp.maximum(top, passed)
            weights.append(jnp.concatenate(parts, axis=0) if n_blocks > 1 else parts[0])
        bound_ref[0] = jnp.max(top)
        for p in pairs:
            acc = lax.dot_general(v_ref[rows, p * LANES:(p + 1) * LANES], weights[p],
                                  (((0,), (0,)), ((), ())), preferred_element_type=_F32)
            acc_ref[slot, p] = acc if first else acc_ref[slot, p] + acc

    def write_out(i, slot):
        rows = pl.ds(pl.multiple_of(i * blk, blk), blk)
        for p in range(n_pairs):
            acc = acc_ref[slot, p]
            out_t = jnp.concatenate([acc[:HEAD_DIM, :blk], acc[HEAD_DIM:, blk:]], axis=0)
            o_ref[rows, p * LANES:(p + 1) * LANES] = out_t.T.astype(o_ref.dtype)

    def stage_queries(i):
        rows = pl.ds(pl.multiple_of(i * blk, blk), blk)
        for p in range(n_pairs):
            q2 = q_ref[rows, p * LANES:(p + 1) * LANES]
            zeros = jnp.zeros_like(q2)
            qs_ref[p] = jnp.concatenate(
                [jnp.where(first_head, q2, zeros), jnp.where(first_head, zeros, q2)], axis=0)

    def query_block(i, carry):
        slot = i & 1
        for n in range(1, STATIC_SWEEP + 1):
            @pl.when((i == n - 1) if n < STATIC_SWEEP else (i >= n - 1))
            def _(n=n):
                stage_queries(i)
                if n > 1:
                    write_out(i - 1, 1 - slot)
                sweep(i, n, first=True, slot=slot)

        def cond(state):
            j, top = state
            return jnp.logical_and(j >= 0, top >= WEIGHT_FLOOR)

        def body(state):
            j, _ = state
            sweep(j, 1, first=False, slot=slot)
            return j - 1, bound_ref[0]

        lax.while_loop(cond, body, (i - STATIC_SWEEP, bound_ref[0]))
        return carry

    n_q = q_ref.shape[0] // blk
    lax.fori_loop(0, n_q, query_block, 0)
    write_out(n_q - 1, (n_q - 1) & 1)


def _attn_call(q, kv, batch, seq):
    m, d = q.shape
    blk = ATTN_BLOCK
    n_pairs = d // LANES
    return pl.pallas_call(
        _attn_kernel,
        grid=(batch,),
        in_specs=[
            pl.BlockSpec((seq, d), lambda b: (b, 0)),
            pl.BlockSpec((seq, d), lambda b: (b, 0)),
            pl.BlockSpec((seq, d), lambda b: (b, 1)),
        ],
        out_specs=pl.BlockSpec((seq, d), lambda b: (b, 0)),
        out_shape=jax.ShapeDtypeStruct((m, d), _BF16),
        scratch_shapes=[
            pltpu.VMEM((n_pairs, 2 * blk, LANES), _BF16),
            pltpu.VMEM((n_pairs, SUBLANES, 2 * blk), _F32),
            pltpu.VMEM((2, n_pairs, LANES, 2 * blk), _F32),
            pltpu.SMEM((1,), _F32),
        ],
        compiler_params=_params(1),
        name="stick_breaking",
    )(q, kv, kv)


def kernel(x, p, ln_mix_a, w_in_a, g_v_a, w_spatial, b_spatial, w_out_a, ln_kv, w_kv, g_k,
           ln_mix_b, w_q, g_q, w_out_b, ln_mlp, w_up, w_down, ln_ple, w_ple_gate, w_ple_proj):
    batch, seq, d = x.shape
    depth = p.shape[0]
    n_a = ln_mix_a.shape[0]
    m = batch * seq
    xf = x.reshape(m, d)
    pf = p.reshape(depth, m, p.shape[-1])
    rows = lambda g: g.reshape(g.shape[0], 1, g.shape[1])
    pairs = lambda g: jnp.concatenate([g, g], axis=-1).reshape(g.shape[0], 1, LANES)
    ln_mix_a, g_v_a, ln_mix_b, ln_mlp, ln_ple = map(rows, (ln_mix_a, g_v_a, ln_mix_b, ln_mlp, ln_ple))
    w_ple_proj = w_ple_proj.astype(_BF16)
    bias = jnp.repeat(jnp.swapaxes(b_spatial, 1, 2), d // b_spatial.shape[1], axis=2)

    def post_weights(i):
        out = (w_out_a, i) if i < n_a else (w_out_b, i - n_a)
        return {("out", i): out, ("up", i): (w_up, i), ("down", i): (w_down, i),
                ("gate", i): (w_ple_gate, i)}

    cast = {}

    def bf16_weight(name, source):
        if name in cast:
            return cast.pop(name)[None], 0
        stacked, layer = source
        return stacked.astype(_BF16), layer

    def ride(call, wanted):
        names = list(wanted)
        main, *casts = call([wanted[n] for n in names])
        cast.update(zip(names, casts))
        return main

    def q_group(j):
        return dict(params=[(ln_mix_b, j), bf16_weight(("q", j), (w_q, j)), (pairs(g_q), j)],
                    n_norm=d // LANES, out_scale=-LOG2_E * HEAD_DIM ** -0.5, segment_major=False)

    kv_shared = q = None
    for i in range(depth):
        projects_kv = i == n_a - 1
        projects_q = n_a <= i + 1 < depth
        proj_weights = {}
        if projects_kv:
            proj_weights[("kv", 0)] = (w_kv[None], 0)
        if projects_q:
            proj_weights[("q", i + 1 - n_a)] = (w_q, i + 1 - n_a)
        if i < n_a:
            mixer_params = [(ln_mix_a, i), (w_in_a, i), (g_v_a, i), (w_spatial, i), (bias, i)]
            wanted = {**(post_weights(i) if i == 0 else {}), **proj_weights}
            proj_weights = {}
            y = ride(lambda riders: _sgu_call(xf, mixer_params, riders), wanted)
        else:
            if q is None:
                q, = _proj_call(xf, [q_group(i - n_a)])
            y = _attn_call(q, kv_shared, batch, seq)
        weights = {n: bf16_weight(n, src) for n, src in post_weights(i).items()}
        post_params = [weights[("out", i)], (ln_mlp, i), weights[("up", i)], weights[("down", i)],
                       (ln_ple, i), weights[("gate", i)], (w_ple_proj, i)]
        next_weights = post_weights(i + 1) if i + 1 < depth else {}
        has_proj = projects_kv or projects_q
        wanted = {**proj_weights, **({} if has_proj else next_weights)}
        xf = ride(lambda riders: _post_call(xf, y, (pf, i), post_params, riders), wanted)
        groups = []
        if projects_q:
            groups.append(q_group(i + 1 - n_a))
        if projects_kv:
            groups.append(dict(
                params=[(rows(ln_kv[None]), 0), bf16_weight(("kv", 0), (w_kv[None], 0)),
                        (pairs(g_k[None]), 0)],
                n_norm=d // LANES, out_scale=1.0, segment_major=True))
        if has_proj:
            names = list(next_weights)
            outs = list(_proj_call(xf, groups, [next_weights[n] for n in names]))
            cast.update(zip(names, outs[len(groups):]))
            q = outs.pop(0) if projects_q else None
            if projects_kv:
                kv_shared = outs.pop(0)
        else:
            q = None
    return xf.reshape(batch, seq, d)
```

```python
import functools

import jax
import jax.numpy as jnp
from jax import lax
from jax.experimental import pallas as pl
from jax.experimental.pallas import tpu as pltpu

EPS = 1e-6
CHUNK = 128
HEAD_DIM = 64
LANES = 128
MXU_WIDTH = 256
BF16_ROWS = 16
TOKEN_TILE = 512
WIDE_TOKEN_TILE = 1024
ATTN_BLOCK = 128
STATIC_SWEEP = 3
assert STATIC_SWEEP >= 2
FF_CHUNK = 1024
VMEM_LIMIT = 56 * 1024 * 1024
SUBLANES = 8
KEY_SEGMENT = ATTN_BLOCK // SUBLANES
LOG2_E = 1.4426950408889634
WEIGHT_FLOOR = 2.0 ** -126

_BF16 = jnp.bfloat16
_F32 = jnp.float32


def _rms(x, g):
    return x * lax.rsqrt(jnp.mean(x * x, axis=-1, keepdims=True) + EPS) * g


def _gelu_tanh(x):
    c = -2.0 * LOG2_E * (2.0 / jnp.pi) ** 0.5
    return x / (1.0 + jnp.exp2(x * (x * x * (c * 0.044715) + c)))


def _layer_spec(param):
    stacked, layer = param
    rest = stacked.shape[1:]
    return pl.BlockSpec((None,) + rest, lambda *_: (layer,) + (0,) * len(rest),
                        pipeline_mode=pl.Buffered(1))


def _rider_specs(riders, n_steps):
    arrays, in_specs, out_specs, out_shapes = [], [], [], []
    for stacked, layer in riders:
        _, r, c = stacked.shape
        rb = r // n_steps
        assert rb * n_steps == r and rb % BF16_ROWS == 0, (stacked.shape, n_steps)
        arrays.append(stacked)
        in_specs.append(pl.BlockSpec((None, rb, c), lambda i, layer=layer: (layer, i, 0)))
        out_specs.append(pl.BlockSpec((rb, c), lambda i: (i, 0)))
        out_shapes.append(jax.ShapeDtypeStruct((r, c), _BF16))
    return arrays, in_specs, out_specs, out_shapes


def _cast_riders(src_refs, dst_refs):
    for src, dst in zip(src_refs, dst_refs):
        dst[...] = src[...].astype(dst.dtype)


def _params(n_axes):
    return pltpu.CompilerParams(
        dimension_semantics=("arbitrary",) * n_axes, vmem_limit_bytes=VMEM_LIMIT)


def _sgu_kernel(x_ref, ln_ref, win_ref, gv_ref, ws_ref, bias_ref, *rest, n_riders):
    y_ref = rest[n_riders]
    _cast_riders(rest[:n_riders], rest[n_riders + 1:])
    tm, d = x_ref.shape
    n_groups = ws_ref.shape[0]
    n_chunks = tm // CHUNK
    h = _rms(x_ref[...], ln_ref[...]).astype(_BF16)
    v = _gelu_tanh(jnp.dot(h, win_ref[:, d:].astype(_BF16), preferred_element_type=_F32))
    v = _rms(v, gv_ref[...]).astype(_BF16)
    u_cols = d // MXU_WIDTH
    u_tiles = [_gelu_tanh(jnp.dot(h, win_ref[:, c * MXU_WIDTH:(c + 1) * MXU_WIDTH].astype(_BF16),
                                  preferred_element_type=_F32)) for c in range(u_cols - 1)]
    row = lax.broadcasted_iota(jnp.int32, (CHUNK, CHUNK), 0)
    col = lax.broadcasted_iota(jnp.int32, (CHUNK, CHUNK), 1)
    causal = col <= row
    mix_cols = []
    for g in range(n_groups):
        w = jnp.where(causal, ws_ref[g], 0.0).astype(_BF16)
        vg = jnp.concatenate(
            [v[c * CHUNK:(c + 1) * CHUNK, g * LANES:(g + 1) * LANES] for c in range(n_chunks)],
            axis=1)
        mg = jnp.dot(w, vg, preferred_element_type=_F32)
        mix_cols.append(jnp.concatenate(
            [mg[:, c * LANES:(c + 1) * LANES] for c in range(n_chunks)], axis=0))
    mix = jnp.concatenate(mix_cols, axis=1)
    bias = jnp.concatenate([bias_ref[...]] * n_chunks, axis=0)
    u_tiles.append(_gelu_tanh(jnp.dot(h, win_ref[:, d - MXU_WIDTH:d].astype(_BF16),
                                      preferred_element_type=_F32)))
    for c, u in enumerate(u_tiles):
        cols = slice(c * MXU_WIDTH, (c + 1) * MXU_WIDTH)
        y_ref[:, cols] = (u * (mix[:, cols] + bias[:, cols])).astype(y_ref.dtype)


def _sgu_call(x, params, riders=()):
    m, d = x.shape
    tm = WIDE_TOKEN_TILE
    tile = pl.BlockSpec((tm, d), lambda i: (i, 0))
    r_arrays, r_in, r_out, r_shapes = _rider_specs(riders, m // tm)
    return pl.pallas_call(
        functools.partial(_sgu_kernel, n_riders=len(riders)),
        grid=(m // tm,),
        in_specs=[tile] + [_layer_spec(a) for a in params] + r_in,
        out_specs=[tile] + r_out,
        out_shape=[jax.ShapeDtypeStruct((m, d), _BF16)] + r_shapes,
        compiler_params=_params(1),
        name="sgu_mixer",
    )(x, *[a for a, _ in params], *r_arrays)


def _post_kernel(x_ref, y_ref, p_ref, wo_ref, lnm_ref, wup_ref, wdn_ref,
                 lnp_ref, wg_ref, wp_ref, *rest, n_riders):
    o_ref = rest[n_riders]
    _cast_riders(rest[:n_riders], rest[n_riders + 1:])
    d_ff = wup_ref.shape[1]
    x = x_ref[...] + jnp.dot(y_ref[...], wo_ref[...], preferred_element_type=_F32)
    h = _rms(x, lnm_ref[...]).astype(_BF16)
    acc = x
    for c in range(d_ff // FF_CHUNK):
        a = jnp.dot(h, wup_ref[:, c * FF_CHUNK:(c + 1) * FF_CHUNK], preferred_element_type=_F32)
        a = jnp.maximum(a, 0.0)
        a = (a * a).astype(_BF16)
        acc = acc + jnp.dot(a, wdn_ref[c * FF_CHUNK:(c + 1) * FF_CHUNK, :],
                            preferred_element_type=_F32)
    x = acc
    hg = _rms(x, lnp_ref[...]).astype(_BF16)
    gate = 1.0 / (1.0 + jnp.exp(-jnp.dot(hg, wg_ref[...], preferred_element_type=_F32)))
    pp = jnp.dot(p_ref[...].astype(_BF16), wp_ref[...], preferred_element_type=_F32)
    o_ref[...] = x + pp * gate


def _post_call(x, y, p_param, params, riders=()):
    m, d = x.shape
    tm = TOKEN_TILE if riders else WIDE_TOKEN_TILE
    tile = pl.BlockSpec((tm, d), lambda i: (i, 0))
    p, p_layer = p_param
    r_arrays, r_in, r_out, r_shapes = _rider_specs(riders, m // tm)
    return pl.pallas_call(
        functools.partial(_post_kernel, n_riders=len(riders)),
        grid=(m // tm,),
        in_specs=[tile, tile, pl.BlockSpec((None, tm, p.shape[2]), lambda i: (p_layer, i, 0))]
        + [_layer_spec(a) for a in params] + r_in,
        out_specs=[tile] + r_out,
        out_shape=[jax.ShapeDtypeStruct((m, d), _F32)] + r_shapes,
        compiler_params=_params(1),
        name="post_mlp_ple",
    )(x, y, p, *[a for a, _ in params], *r_arrays)


def _proj_kernel(x_ref, *refs, groups, n_riders):
    n_in = 3 * len(groups)
    outs = refs[n_in + n_riders:]
    _cast_riders(refs[n_in:n_in + n_riders], outs[len(groups):])
    x = x_ref[...]
    for n, cfg in enumerate(groups):
        _project_heads(x, *refs[3 * n:3 * n + 3], outs[n], **cfg)


def _project_heads(x, ln_ref, w_ref, g_ref, o_ref, *, n_norm, out_scale, segment_major):
    tm = x.shape[0]
    h = _rms(x, ln_ref[...]).astype(_BF16)
    if segment_major:
        r = lax.broadcasted_iota(jnp.int32, (ATTN_BLOCK, ATTN_BLOCK), 0)
        c = lax.broadcasted_iota(jnp.int32, (ATTN_BLOCK, ATTN_BLOCK), 1)
        src = KEY_SEGMENT * (r & (SUBLANES - 1)) + r // SUBLANES
        perm = jnp.where(c == src, 1.0, 0.0).astype(_BF16)
        h = jnp.concatenate(
            [jnp.dot(perm, h[b * ATTN_BLOCK:(b + 1) * ATTN_BLOCK],
                     preferred_element_type=_F32).astype(_BF16)
             for b in range(tm // ATTN_BLOCK)], axis=0)
    first_head = lax.broadcasted_iota(jnp.int32, (tm, LANES), 1) < HEAD_DIM
    gain = g_ref[...] * out_scale
    for c in range(w_ref.shape[1] // MXU_WIDTH):
        y = jnp.dot(h, w_ref[:, c * MXU_WIDTH:(c + 1) * MXU_WIDTH], preferred_element_type=_F32)
        for t in range(MXU_WIDTH // LANES):
            blk = y[:, t * LANES:(t + 1) * LANES]
            col = c * MXU_WIDTH + t * LANES
            if col < n_norm * LANES:
                sq = blk * blk
                s_a = jnp.sum(jnp.where(first_head, sq, 0.0), axis=-1, keepdims=True)
                s_b = jnp.sum(jnp.where(first_head, 0.0, sq), axis=-1, keepdims=True)
                ms = jnp.where(first_head, s_a, s_b) * (1.0 / HEAD_DIM)
                blk = blk * (lax.rsqrt(ms + EPS) * gain)
            o_ref[:, col:col + LANES] = blk.astype(o_ref.dtype)


def _proj_call(x, groups, riders=()):
    m, d = x.shape
    tm = WIDE_TOKEN_TILE
    tile = lambda width: pl.BlockSpec((tm, width), lambda i: (i, 0))
    params = [a for g in groups for a in g["params"]]
    widths = [g["params"][1][0].shape[2] for g in groups]
    cfgs = tuple({k: g[k] for k in ("n_norm", "out_scale", "segment_major")} for g in groups)
    r_arrays, r_in, r_out, r_shapes = _rider_specs(riders, m // tm)
    return pl.pallas_call(
        functools.partial(_proj_kernel, groups=cfgs, n_riders=len(riders)),
        grid=(m // tm,),
        in_specs=[tile(d)] + [_layer_spec(a) for a in params] + r_in,
        out_specs=[tile(n) for n in widths] + r_out,
        out_shape=[jax.ShapeDtypeStruct((m, n), _BF16) for n in widths] + r_shapes,
        compiler_params=_params(1),
        name="head_proj",
    )(x, *[a for a, _ in params], *r_arrays)


def _attn_kernel(q_ref, k_ref, v_ref, o_ref, qs_ref, passed_ref, acc_ref, bound_ref):
    blk = ATTN_BLOCK
    n_pairs = qs_ref.shape[0]
    n_groups = blk // SUBLANES
    lane = lax.broadcasted_iota(jnp.int32, (blk, LANES), 1)
    first_head = lane < HEAD_DIM
    sub = lax.broadcasted_iota(jnp.int32, (SUBLANES, 2 * blk), 0)

    def block_weights(z, passed, valid):
        beta = 1.0 / (1.0 + jnp.exp2(z))
        keep = 1.0 - beta
        sub = lax.broadcasted_iota(jnp.int32, (SUBLANES, z.shape[1]), 0)
        if valid is not None:
            keep = jnp.where(valid, keep, 1.0)
        after = [None] * n_groups
        run = None
        for v in range(n_groups - 1, -1, -1):
            after[v] = run
            kv = keep[v * SUBLANES:(v + 1) * SUBLANES]
            run = kv if run is None else run * kv
        later = jnp.where(sub + 1 < SUBLANES, pltpu.roll(run, SUBLANES - 1, 0), 1.0)
        for step in (1, 2, 4):
            shifted = pltpu.roll(later, SUBLANES - step, 0)
            later = later * jnp.where(sub + step < SUBLANES, shifted, 1.0)
        whole = later * run
        base = later if passed is None else later * passed
        parts = []
        for v in range(n_groups):
            scale = base if after[v] is None else after[v] * base
            parts.append(beta[v * SUBLANES:(v + 1) * SUBLANES] * scale)
        w = jnp.concatenate(parts, axis=0)
        if valid is not None:
            w = jnp.where(valid, w, 0.0)
        total = jnp.broadcast_to(whole[0:1], whole.shape)
        return w.astype(_BF16), total if passed is None else passed * total

    def sweep(j_last, n_blocks, first, slot):
        rows = pl.ds(pl.multiple_of((j_last - (n_blocks - 1)) * blk, blk), n_blocks * blk)
        valid = None
        if first:
            r = lax.broadcasted_iota(jnp.int32, (blk, 2 * blk), 0)
            key = KEY_SEGMENT * (r & (SUBLANES - 1)) + r // SUBLANES
            query = lax.broadcasted_iota(jnp.int32, (blk, 2 * blk), 1) & (blk - 1)
            valid = key < query
        pairs = range(n_pairs)
        zs = [lax.dot_general(k_ref[rows, p * LANES:(p + 1) * LANES], qs_ref[p],
                              (((1,), (1,)), ((), ())), preferred_element_type=_F32)
              for p in pairs]
        weights = []
        top = None
        for p in pairs:
            before = None if first else passed_ref[p]
            by_head = []
            for lanes in (slice(0, blk), slice(blk, 2 * blk)):
                passed = None if first else before[:, lanes]
                parts = [None] * n_blocks
                for b in range(n_blocks - 1, -1, -1):
                    parts[b], passed = block_weights(
                        zs[p][b * blk:(b + 1) * blk, lanes], passed,
                        valid[:, lanes] if first and b == n_blocks - 1 else None)
                by_head.append(
                    (jnp.concatenate(parts, axis=0) if n_blocks > 1 else parts[0], passed))
            passed = jnp.concatenate([h[1] for h in by_head], axis=1)
            passed_ref[p] = passed
            top = passed if top is None else jnp.maximum(top, passed)
            weights.append(jnp.concatenate([h[0] for h in by_head], axis=1))
        bound_ref[0] = jnp.max(top)
        for p in pairs:
            acc = lax.dot_general(v_ref[rows, p * LANES:(p + 1) * LANES], weights[p],
                                  (((0,), (0,)), ((), ())), preferred_element_type=_F32)
            acc_ref[slot, p] = acc if first else acc_ref[slot, p] + acc

    def write_out(i, slot):
        rows = pl.ds(pl.multiple_of(i * blk, blk), blk)
        for p in range(n_pairs):
            acc = acc_ref[slot, p]
            out_t = jnp.concatenate([acc[:HEAD_DIM, :blk], acc[HEAD_DIM:, blk:]], axis=0)
            o_ref[rows, p * LANES:(p + 1) * LANES] = out_t.T.astype(o_ref.dtype)

    def query_block(i, carry):
        rows = pl.ds(pl.multiple_of(i * blk, blk), blk)
        slot = i & 1
        for p in range(n_pairs):
            q2 = q_ref[rows, p * LANES:(p + 1) * LANES]
            zeros = jnp.zeros_like(q2)
            qs_ref[p] = jnp.concatenate(
                [jnp.where(first_head, q2, zeros), jnp.where(first_head, zeros, q2)], axis=0)

        for n in range(1, STATIC_SWEEP + 1):
            @pl.when((i == n - 1) if n < STATIC_SWEEP else (i >= n - 1))
            def _(n=n):
                if n > 1:
                    write_out(i - 1, 1 - slot)
                sweep(i, n, first=True, slot=slot)

        def cond(state):
            j, top = state
            return jnp.logical_and(j >= 0, top >= WEIGHT_FLOOR)

        def body(state):
            j, _ = state
            sweep(j, 1, first=False, slot=slot)
            return j - 1, bound_ref[0]

        lax.while_loop(cond, body, (i - STATIC_SWEEP, bound_ref[0]))
        return carry

    n_q = q_ref.shape[0] // blk
    lax.fori_loop(0, n_q, query_block, 0)
    write_out(n_q - 1, (n_q - 1) & 1)


def _attn_call(q, kv, batch, seq):
    m, d = q.shape
    blk = ATTN_BLOCK
    n_pairs = d // LANES
    return pl.pallas_call(
        _attn_kernel,
        grid=(batch,),
        in_specs=[
            pl.BlockSpec((seq, d), lambda b: (b, 0)),
            pl.BlockSpec((seq, d), lambda b: (b, 0)),
            pl.BlockSpec((seq, d), lambda b: (b, 1)),
        ],
        out_specs=pl.BlockSpec((seq, d), lambda b: (b, 0)),
        out_shape=jax.ShapeDtypeStruct((m, d), _BF16),
        scratch_shapes=[
            pltpu.VMEM((n_pairs, 2 * blk, LANES), _BF16),
            pltpu.VMEM((n_pairs, SUBLANES, 2 * blk), _F32),
            pltpu.VMEM((2, n_pairs, LANES, 2 * blk), _F32),
            pltpu.SMEM((1,), _F32),
        ],
        compiler_params=_params(1),
        name="stick_breaking",
    )(q, kv, kv)


def kernel(x, p, ln_mix_a, w_in_a, g_v_a, w_spatial, b_spatial, w_out_a, ln_kv, w_kv, g_k,
           ln_mix_b, w_q, g_q, w_out_b, ln_mlp, w_up, w_down, ln_ple, w_ple_gate, w_ple_proj):
    batch, seq, d = x.shape
    depth = p.shape[0]
    n_a = ln_mix_a.shape[0]
    m = batch * seq
    xf = x.reshape(m, d)
    pf = p.reshape(depth, m, p.shape[-1])
    rows = lambda g: g.reshape(g.shape[0], 1, g.shape[1])
    pairs = lambda g: jnp.concatenate([g, g], axis=-1).reshape(g.shape[0], 1, LANES)
    ln_mix_a, g_v_a, ln_mix_b, ln_mlp, ln_ple = map(rows, (ln_mix_a, g_v_a, ln_mix_b, ln_mlp, ln_ple))
    w_ple_proj = w_ple_proj.astype(_BF16)
    bias = jnp.repeat(jnp.swapaxes(b_spatial, 1, 2), d // b_spatial.shape[1], axis=2)

    def post_weights(i):
        out = (w_out_a, i) if i < n_a else (w_out_b, i - n_a)
        return {("out", i): out, ("up", i): (w_up, i), ("down", i): (w_down, i),
                ("gate", i): (w_ple_gate, i)}

    cast = {}

    def bf16_weight(name, source):
        if name in cast:
            return cast.pop(name)[None], 0
        stacked, layer = source
        return stacked.astype(_BF16), layer

    def ride(call, wanted):
        names = list(wanted)
        main, *casts = call([wanted[n] for n in names])
        cast.update(zip(names, casts))
        return main

    def q_group(j):
        return dict(params=[(ln_mix_b, j), bf16_weight(("q", j), (w_q, j)), (pairs(g_q), j)],
                    n_norm=d // LANES, out_scale=-LOG2_E * HEAD_DIM ** -0.5, segment_major=False)

    kv_shared = q = None
    for i in range(depth):
        projects_kv = i == n_a - 1
        projects_q = n_a <= i + 1 < depth
        proj_weights = {}
        if projects_kv:
            proj_weights[("kv", 0)] = (w_kv[None], 0)
        if projects_q:
            proj_weights[("q", i + 1 - n_a)] = (w_q, i + 1 - n_a)
        if i < n_a:
            mixer_params = [(ln_mix_a, i), (w_in_a, i), (g_v_a, i), (w_spatial, i), (bias, i)]
            wanted = {**(post_weights(i) if i == 0 else {}), **proj_weights}
            proj_weights = {}
            y = ride(lambda riders: _sgu_call(xf, mixer_params, riders), wanted)
        else:
            if q is None:
                q, = _proj_call(xf, [q_group(i - n_a)])
            y = _attn_call(q, kv_shared, batch, seq)
        weights = {n: bf16_weight(n, src) for n, src in post_weights(i).items()}
        post_params = [weights[("out", i)], (ln_mlp, i), weights[("up", i)], weights[("down", i)],
                       (ln_ple, i), weights[("gate", i)], (w_ple_proj, i)]
        next_weights = post_weights(i + 1) if i + 1 < depth else {}
        has_proj = projects_kv or projects_q
        wanted = {**proj_weights, **({} if has_proj else next_weights)}
        xf = ride(lambda riders: _post_call(xf, y, (pf, i), post_params, riders), wanted)
        groups = []
        if projects_q:
            groups.append(q_group(i + 1 - n_a))
        if projects_kv:
            groups.append(dict(
                params=[(rows(ln_kv[None]), 0), bf16_weight(("kv", 0), (w_kv[None], 0)),
                        (pairs(g_k[None]), 0)],
                n_norm=d // LANES, out_scale=1.0, segment_major=True))
        if has_proj:
            names = list(next_weights)
            outs = list(_proj_call(xf, groups, [next_weights[n] for n in names]))
            cast.update(zip(names, outs[len(groups):]))
            q = outs.pop(0) if projects_q else None
            if projects_kv:
                kv_shared = outs.pop(0)
        else:
            q = None
    return xf.reshape(batch, seq, d)
```
